```python
import math
import jax, jax.numpy as jnp
from jax import lax
import numpy as np

D_MODEL = 1024
BATCH = 4
SEQ = 4096
DEPTH = 2

HEAD_DIM = 64
N_HEADS_A = 8
A_QK_DIM = 32
A_V_DIM = 2 * A_QK_DIM
N_HEADS_B = 8
N_HEADS_C = 8
N_HEADS_D = 8
A_QK_W = N_HEADS_A * 2 * A_QK_DIM
A_V_W = N_HEADS_A * A_V_DIM
B_W = N_HEADS_B * HEAD_DIM
C_W = N_HEADS_C * HEAD_DIM
D_W = N_HEADS_D * HEAD_DIM
AB_SPLITS = (A_QK_W, A_QK_W, A_V_W, B_W, B_W, B_W)
CD_SPLITS = (C_W, C_W, C_W, D_W, D_W, D_W, N_HEADS_D)
AB_IN = 2 * A_QK_W + A_V_W + 3 * B_W
AB_OUT = A_V_W + B_W
CD_IN = 3 * C_W + 3 * D_W + N_HEADS_D
CD_OUT = C_W + D_W
N_EVEN = (DEPTH + 1) // 2
N_ODD = DEPTH // 2
Q_BLOCK = 128
T5_BUCKETS = 32
T5_MAX_DIST = 128
N_T5_HEADS = 8
DILATED_PATTERNS = ((128, 1), (512, 4), (2048, 16))
PEER_HEADS = 8
PEER_NKEYS = 128
PEER_EXPERTS = PEER_NKEYS * PEER_NKEYS
PEER_DKEY = 256
PEER_TOPK = 16
PEER_TOKEN_BLOCK = 128
DN_ALPHA = (2 * DEPTH) ** 0.25
DN_BETA = (8 * DEPTH) ** -0.25
LN_EPS = 1e-5

kernel_name = "hybrid_diff_stickbreak_dilated_fox_peer"


def layer_norm(x, g, b):
    xf = x.astype(jnp.float32)
    mu = jnp.mean(xf, axis=-1, keepdims=True)
    var = jnp.mean(jnp.square(xf - mu), axis=-1, keepdims=True)
    return ((xf - mu) * lax.rsqrt(var + LN_EPS) * g + b).astype(x.dtype)


def split_cols(t, sizes):
    out, start = [], 0
    for n in sizes:
        out.append(t[..., start:start + n])
        start += n
    return out


def to_heads(t, n_heads):
    B, S, _ = t.shape
    return t.reshape(B, S, n_heads, -1).transpose(0, 2, 1, 3)


def merge_heads(o):
    B, H, S, d = o.shape
    return o.transpose(0, 2, 1, 3).reshape(B, S, H * d)


def to_q_blocks(t):
    B, H, S = t.shape[:3]
    t = t.reshape((B, H, S // Q_BLOCK, Q_BLOCK) + t.shape[3:])
    return jnp.moveaxis(t, 2, 0)


def from_q_blocks(o):
    o = jnp.moveaxis(o, 0, 2)
    return o.reshape(o.shape[0], o.shape[1], -1, o.shape[-1])


def t5_bucket(dist):
    max_exact = T5_BUCKETS // 2
    n = jnp.maximum(dist, 0)
    nf = jnp.maximum(n, 1).astype(jnp.float32)
    large = max_exact + (jnp.log(nf / max_exact) / math.log(T5_MAX_DIST / max_exact)
                         * (T5_BUCKETS - max_exact)).astype(jnp.int32)
    large = jnp.minimum(large, T5_BUCKETS - 1)
    return jnp.where(n < max_exact, n, large)


def diff_attention(q, k, v, t5_bias, lam, lambda_init, subln_g):
    S = q.shape[2]
    scale = A_QK_DIM ** -0.5
    kpos = jnp.arange(S)

    def block(args):
        i, qi = args
        qpos = i * Q_BLOCK + jnp.arange(Q_BLOCK)
        dist = qpos[:, None] - kpos[None, :]
        bias = jnp.moveaxis(t5_bias[t5_bucket(dist)], -1, 0).astype(jnp.float32)
        logits = jnp.einsum('bhqcd,bhscd->bhcqs', qi, k).astype(jnp.float32) * scale
        logits = logits + bias[None, :, None]
        logits = jnp.where(dist >= 0, logits, -jnp.inf)
        p = jax.nn.softmax(logits, axis=-1)
        w = p[:, :, 0] - lam * p[:, :, 1]
        return jnp.einsum('bhqs,bhsd->bhqd', w.astype(v.dtype), v)

    nb = S // Q_BLOCK
    o = from_q_blocks(lax.map(block, (jnp.arange(nb), to_q_blocks(q))))
    of = o.astype(jnp.float32)
    of = of * lax.rsqrt(jnp.mean(jnp.square(of), axis=-1, keepdims=True) + LN_EPS)
    return (of * subln_g * (1.0 - lambda_init)).astype(v.dtype)


def stick_breaking_attention(q, k, v):
    S, d = q.shape[2], q.shape[3]
    scale = d ** -0.5
    kpos = jnp.arange(S)

    def block(args):
        i, qi = args
        qpos = i * Q_BLOCK + jnp.arange(Q_BLOCK)
        past = kpos[None, :] < qpos[:, None]
        z = jnp.einsum('bhqd,bhsd->bhqs', qi, k).astype(jnp.float32) * scale
        log_beta = jax.nn.log_sigmoid(z)
        log_rem = jnp.where(past, jax.nn.log_sigmoid(-z), 0.0)
        after = lax.cumsum(log_rem, axis=3, reverse=True) - log_rem
        w = jnp.where(past, jnp.exp(log_beta + after), 0.0)
        return jnp.einsum('bhqs,bhsd->bhqd', w.astype(v.dtype), v)

    nb = S // Q_BLOCK
    return from_q_blocks(lax.map(block, (jnp.arange(nb), to_q_blocks(q))))


def dilated_attention(q, k, v, t5_bias):
    B, H, S, d = q.shape
    scale = d ** -0.5

    def block(args):
        i, qi = args
        qpos = i * Q_BLOCK + jnp.arange(Q_BLOCK)
        ms, ss, os_ = [], [], []
        for window, dil in DILATED_PATTERNS:
            n_keys = window // dil + 1
            dist = jnp.arange(n_keys) * dil
            kidx = qpos[:, None] - dist[None, :]
            valid = kidx >= 0
            kidx = jnp.maximum(kidx, 0).reshape(-1)
            kg = jnp.take(k, kidx, axis=2).reshape(B, H, Q_BLOCK, n_keys, d)
            vg = jnp.take(v, kidx, axis=2).reshape(B, H, Q_BLOCK, n_keys, d)
            bias = t5_bias[t5_bucket(dist)].T.astype(jnp.float32)
            logits = jnp.einsum('bhqd,bhqjd->bhqj', qi, kg).astype(jnp.float32) * scale
            logits = jnp.where(valid, logits + bias[None, :, None, :], -jnp.inf)
            m = jnp.max(logits, axis=-1)
            e = jnp.exp(logits - m[..., None])
            s = jnp.sum(e, axis=-1)
            o = jnp.einsum('bhqj,bhqjd->bhqd', e, vg.astype(jnp.float32)) / s[..., None]
            ms.append(m)
            ss.append(s)
            os_.append(o)
        m_all = jnp.maximum(jnp.maximum(ms[0], ms[1]), ms[2])
        wts = [s * jnp.exp(m - m_all) for m, s in zip(ms, ss)]
        num = sum(w[..., None] * o for w, o in zip(wts, os_))
        den = sum(wts)
        return (num / den[..., None]).astype(q.dtype)

    nb = S // Q_BLOCK
    return from_q_blocks(lax.map(block, (jnp.arange(nb), to_q_blocks(q))))


def forgetting_attention(q, k, v, log_f):
    S, d = q.shape[2], q.shape[3]
    scale = d ** -0.5
    kpos = jnp.arange(S)
    F = jnp.cumsum(log_f, axis=-1)

    def block(args):
        i, qi, Fi = args
        qpos = i * Q_BLOCK + jnp.arange(Q_BLOCK)
        causal = kpos[None, :] <= qpos[:, None]
        logits = jnp.einsum('bhqd,bhsd->bhqs', qi, k).astype(jnp.float32) * scale
        logits = logits + Fi[..., :, None] - F[:, :, None, :]
        logits = jnp.where(causal, logits, -jnp.inf)
        p = jax.nn.softmax(logits, axis=-1)
        return jnp.einsum('bhqs,bhsd->bhqd', p.astype(v.dtype), v)

    nb = S // Q_BLOCK
    return from_q_blocks(lax.map(block, (jnp.arange(nb), to_q_blocks(q), to_q_blocks(F))))


def ab_mixer(h, t5_bias, w_in, lam_vecs, subln_g, w_out, lambda_init):
    B, S, _ = h.shape
    aq, ak, av, bq, bk, bv = split_cols(h @ w_in, AB_SPLITS)
    aq = aq.reshape(B, S, N_HEADS_A, 2, A_QK_DIM).transpose(0, 2, 1, 3, 4)
    ak = ak.reshape(B, S, N_HEADS_A, 2, A_QK_DIM).transpose(0, 2, 1, 3, 4)
    lv = lam_vecs.astype(jnp.float32)
    lam = jnp.exp(jnp.sum(lv[0] * lv[1])) - jnp.exp(jnp.sum(lv[2] * lv[3])) + lambda_init
    oa = diff_attention(aq, ak, to_heads(av, N_HEADS_A), t5_bias, lam, lambda_init, subln_g)
    ob = stick_breaking_attention(to_heads(bq, N_HEADS_B), to_heads(bk, N_HEADS_B), to_heads(bv, N_HEADS_B))
    return jnp.concatenate([merge_heads(oa), merge_heads(ob)], axis=-1) @ w_out


def cd_mixer(h, t5_bias, w_in, b_f, w_out):
    cq, ck, cv, dq, dk, dv, fg = split_cols(h @ w_in, CD_SPLITS)
    oc = dilated_attention(to_heads(cq, N_HEADS_C), to_heads(ck, N_HEADS_C), to_heads(cv, N_HEADS_C), t5_bias)
    log_f = jax.nn.log_sigmoid(fg.astype(jnp.float32) + b_f).transpose(0, 2, 1)
    od = forgetting_attention(to_heads(dq, N_HEADS_D), to_heads(dk, N_HEADS_D), to_heads(dv, N_HEADS_D), log_f)
    return jnp.concatenate([merge_heads(oc), merge_heads(od)], axis=-1) @ w_out


def peer(x, w_q, sub_keys, u, v):
    B, S, D = x.shape
    xt = x.reshape(B * S // PEER_TOKEN_BLOCK, PEER_TOKEN_BLOCK, D)

    def block(xb):
        q = (xb @ w_q).reshape(PEER_TOKEN_BLOCK, PEER_HEADS, 2, PEER_DKEY // 2)
        sc = jnp.einsum('thcd,hckd->thck', q, sub_keys).astype(jnp.float32)
        top_s, top_i = lax.top_k(sc, PEER_TOPK)
        cand_s = top_s[:, :, 0, :, None] + top_s[:, :, 1, None, :]
        cand_i = top_i[:, :, 0, :, None] * PEER_NKEYS + top_i[:, :, 1, None, :]
        cand_s = cand_s.reshape(PEER_TOKEN_BLOCK, PEER_HEADS, PEER_TOPK * PEER_TOPK)
        cand_i = cand_i.reshape(PEER_TOKEN_BLOCK, PEER_HEADS, PEER_TOPK * PEER_TOPK)
        best_s, pos = lax.top_k(cand_s, PEER_TOPK)
        idx = jnp.take_along_axis(cand_i, pos, axis=-1)
        g = jax.nn.softmax(best_s, axis=-1)
        ug = jnp.take(u, idx, axis=0)
        vg = jnp.take(v, idx, axis=0)
        act = jax.nn.gelu(jnp.einsum('thkd,td->thk', ug, xb), approximate=False)
        return jnp.einsum('thk,thkd->td', (g * act.astype(jnp.float32)).astype(vg.dtype), vg)

    return lax.map(block, xt).reshape(B, S, D)


def _col_scale(sizes, value_slots):
    return jnp.concatenate([jnp.full((n,), DN_BETA if i in value_slots else 1.0, jnp.float32)
                            for i, n in enumerate(sizes)])


def setup_inputs(seed: int = 0) -> dict:
    key = jax.random.key(seed)
    ks = jax.random.split(key, 18)

    def nrm(k, shape, s):
        return jax.random.normal(k, shape, jnp.float32) * s

    x = nrm(ks[0], (BATCH, SEQ, D_MODEL), 1.0)
    t5_bias = nrm(ks[1], (T5_BUCKETS, N_T5_HEADS), 0.5)
    ab_w_in = nrm(ks[2], (N_EVEN, D_MODEL, AB_IN), D_MODEL ** -0.5) * _col_scale(AB_SPLITS, (2, 5))
    ab_lambda = nrm(ks[3], (N_EVEN, 4, A_QK_DIM), 0.1)
    ab_subln_g = 1.0 + nrm(ks[4], (N_EVEN, A_V_DIM), 0.02)
    ab_w_out = nrm(ks[5], (N_EVEN, AB_OUT, D_MODEL), AB_OUT ** -0.5 * DN_BETA)
    cd_w_in = nrm(ks[6], (N_ODD, D_MODEL, CD_IN), D_MODEL ** -0.5) * _col_scale(CD_SPLITS, (2, 5))
    cd_b_f = 3.0 + nrm(ks[7], (N_ODD, N_HEADS_D), 0.1)
    cd_w_out = nrm(ks[8], (N_ODD, CD_OUT, D_MODEL), CD_OUT ** -0.5 * DN_BETA)
    peer_w_q = nrm(ks[9], (DEPTH, D_MODEL, PEER_HEADS * PEER_DKEY), D_MODEL ** -0.5)
    peer_sub_keys = nrm(ks[10], (DEPTH, PEER_HEADS, 2, PEER_NKEYS, PEER_DKEY // 2), (PEER_DKEY // 2) ** -0.5)
    peer_u = nrm(ks[11], (DEPTH, PEER_EXPERTS, D_MODEL), D_MODEL ** -0.5)
    peer_v = nrm(ks[12], (DEPTH, PEER_EXPERTS, D_MODEL), DN_BETA)
    ln_g = 1.0 + nrm(ks[13], (DEPTH, 2, D_MODEL), 0.02)
    ln_b = nrm(ks[14], (DEPTH, 2, D_MODEL), 0.02)
    return {"x": x, "t5_bias": t5_bias, "ab_w_in": ab_w_in, "ab_lambda": ab_lambda,
            "ab_subln_g": ab_subln_g, "ab_w_out": ab_w_out, "cd_w_in": cd_w_in, "cd_b_f": cd_b_f,
            "cd_w_out": cd_w_out, "peer_w_q": peer_w_q, "peer_sub_keys": peer_sub_keys,
            "peer_u": peer_u, "peer_v": peer_v, "ln_g": ln_g, "ln_b": ln_b}


def reference(x, t5_bias, ab_w_in, ab_lambda, ab_subln_g, ab_w_out, cd_w_in, cd_b_f, cd_w_out,
              peer_w_q, peer_sub_keys, peer_u, peer_v, ln_g, ln_b):
    h = x
    for layer in range(DEPTH):
        j = layer // 2
        if layer % 2 == 0:
            lambda_init = 0.8 - 0.6 * math.exp(-0.3 * layer)
            mix = ab_mixer(h, t5_bias, ab_w_in[j], ab_lambda[j], ab_subln_g[j], ab_w_out[j], lambda_init)
        else:
            mix = cd_mixer(h, t5_bias, cd_w_in[j], cd_b_f[j], cd_w_out[j])
        h = layer_norm(DN_ALPHA * h + mix, ln_g[layer, 0], ln_b[layer, 0])
        ffn = peer(h, peer_w_q[layer], peer_sub_keys[layer], peer_u[layer], peer_v[layer])
        h = layer_norm(DN_ALPHA * h + ffn, ln_g[layer, 1], ln_b[layer, 1])
    return h
```

```python
import functools
import math

import numpy as np
import jax
import jax.numpy as jnp
from jax import lax
from jax.experimental import pallas as pl
from jax.experimental.pallas import tpu as pltpu

F32 = jnp.float32
BF16 = jnp.bfloat16

HEAD_DIM = 64
N_HEADS = 8
A_QK_DIM = 32
T5_BUCKETS = 32
T5_MAX_DIST = 128
DILATED_PATTERNS = ((128, 1), (512, 4), (2048, 16))
PEER_HEADS = 8
PEER_NKEYS = 128
PEER_TOPK = 16
LN_EPS = 1e-5

LANES = 128
HEADS_PER_TILE = LANES // HEAD_DIM
N_PAIRS = N_HEADS // HEADS_PER_TILE
TQ = 256
TK = 256
ROW_TILE = 512
PEER_SEL_TOKENS = 256
PEER_TOKENS = 512
PEER_ROWS_PER_CHUNK = 8
VMEM_LIMIT_BYTES = 56 * 1024 * 1024
NEG = -1e30
SB_CUTOFF = -110.0


def _nt_dot(a, b):
    return lax.dot_general(a, b, (((1,), (1,)), ((), ())), preferred_element_type=F32)


def _dot(a, b):
    return jnp.dot(a, b, preferred_element_type=F32)


def _params(*sem):
    return pltpu.CompilerParams(dimension_semantics=sem, vmem_limit_bytes=VMEM_LIMIT_BYTES)


def _layer_norm(y, g, b):
    mu = jnp.mean(y, axis=-1, keepdims=True)
    yc = y - mu
    var = jnp.mean(yc * yc, axis=-1, keepdims=True)
    return yc * lax.rsqrt(var + LN_EPS) * g + b


def _log_sigmoid_pair(z):
    sp = jnp.log(1.0 + jnp.exp(-jnp.abs(z)))
    return jnp.minimum(z, 0.0) - sp, -jnp.maximum(z, 0.0) - sp


def _proj_kernel(x_ref, w_ref, o_ref, *, tn):
    x = x_ref[...].astype(BF16)
    for c in range(w_ref.shape[1] // tn):
        o_ref[:, c * tn:(c + 1) * tn] = _dot(x, w_ref[:, c * tn:(c + 1) * tn]).astype(o_ref.dtype)


def _proj_gate_kernel(x_ref, w_ref, wg_ref, o_ref, g_ref, *, tn):
    x = x_ref[...].astype(BF16)
    for c in range(w_ref.shape[1] // tn):
        o_ref[:, c * tn:(c + 1) * tn] = _dot(x, w_ref[:, c * tn:(c + 1) * tn]).astype(o_ref.dtype)
    g_ref[...] = _nt_dot(wg_ref[...], x)


def _project(x, w, wg_t=None, tn=512):
    n, k = x.shape
    m = w.shape[1]
    tm = ROW_TILE
    x_spec = pl.BlockSpec((tm, k), lambda i: (i, 0))
    w_spec = pl.BlockSpec((k, m), lambda i: (0, 0))
    o_spec = pl.BlockSpec((tm, m), lambda i: (i, 0))
    if wg_t is None:
        return pl.pallas_call(
            functools.partial(_proj_kernel, tn=tn),
            out_shape=jax.ShapeDtypeStruct((n, m), BF16),
            grid=(n // tm,), in_specs=[x_spec, w_spec], out_specs=o_spec,
            compiler_params=_params("parallel"), name="proj")(x, w)
    ng = wg_t.shape[0]
    return pl.pallas_call(
        functools.partial(_proj_gate_kernel, tn=tn),
        out_shape=(jax.ShapeDtypeStruct((n, m), BF16), jax.ShapeDtypeStruct((ng, n), F32)),
        grid=(n // tm,),
        in_specs=[x_spec, w_spec, pl.BlockSpec((ng, k), lambda i: (0, 0))],
        out_specs=(o_spec, pl.BlockSpec((ng, tm), lambda i: (0, i))),
        compiler_params=_params("parallel"), name="proj_gate")(x, w, wg_t)


def _mix_ln_kernel(oa_ref, ob_ref, wa_ref, wb_ref, h_ref, g_ref, b_ref, o_ref, *, alpha):
    mix = _dot(oa_ref[...], wa_ref[...]) + _dot(ob_ref[...], wb_ref[...])
    o_ref[...] = _layer_norm(alpha * h_ref[...] + mix, g_ref[...], b_ref[...])


def _mix_ln(oa, ob, w_out, h, g, b, alpha):
    n, d = h.shape
    wa_rows = oa.shape[1]
    tm = ROW_TILE
    row = lambda width: pl.BlockSpec((tm, width), lambda i: (i, 0))
    full = lambda r, c: pl.BlockSpec((r, c), lambda i: (0, 0))
    return pl.pallas_call(
        functools.partial(_mix_ln_kernel, alpha=alpha),
        out_shape=jax.ShapeDtypeStruct((n, d), F32),
        grid=(n // tm,),
        in_specs=[row(wa_rows), row(ob.shape[1]), full(wa_rows, d), full(ob.shape[1], d),
                  row(d), full(1, d), full(1, d)],
        out_specs=row(d),
        compiler_params=_params("parallel"), name="mix_ln",
    )(oa, ob, w_out[:wa_rows], w_out[wa_rows:], h, g.reshape(1, d), b.reshape(1, d))


def _softmax_step(s, v, m_ref, l_ref, acc_ref):
    m_old = m_ref[...]
    m_new = jnp.maximum(m_old, jnp.max(s, axis=-1, keepdims=True))
    alpha = jnp.exp(m_old - m_new)
    p = jnp.exp(s - m_new)
    l_ref[...] = alpha * l_ref[...] + jnp.sum(p, axis=-1, keepdims=True)
    acc_ref[...] = alpha * acc_ref[...] + _dot(p.astype(BF16), v)
    m_ref[...] = m_new


def _lane_group_mask(shape, width, index):
    lane = lax.broadcasted_iota(jnp.int32, shape, len(shape) - 1)
    return (lane // width) == index


def _kv_block(k_ref, v_ref, j):
    start = pl.multiple_of(j * TK, TK)
    return k_ref[pl.ds(start, TK), :], v_ref[pl.ds(start, TK), :]


def _attn_specs(q_col, k_col, v_col, seq):
    q_spec = pl.BlockSpec((None, TQ, LANES), lambda b, hp, qi: (b, qi, q_col + hp))
    k_spec = pl.BlockSpec((None, seq, LANES), lambda b, hp, qi: (b, 0, k_col + hp))
    v_spec = pl.BlockSpec((None, seq, LANES), lambda b, hp, qi: (b, 0, v_col + hp))
    o_spec = pl.BlockSpec((None, TQ, LANES), lambda b, hp, qi: (b, qi, hp))
    return q_spec, k_spec, v_spec, o_spec


def _attn_out_shape(p):
    return jax.ShapeDtypeStruct((p.shape[0], p.shape[1], N_HEADS * HEAD_DIM), BF16)


def _softmax_scratch(n):
    return [pltpu.VMEM((n, TQ, 1), F32), pltpu.VMEM((n, TQ, 1), F32), pltpu.VMEM((n, TQ, LANES), F32)]


def _init_softmax(m_ref, l_ref, acc_ref):
    m_ref[...] = jnp.full(m_ref.shape, NEG, F32)
    l_ref[...] = jnp.zeros(l_ref.shape, F32)
    acc_ref[...] = jnp.zeros(acc_ref.shape, F32)


def _pair_select(first, second):
    return jnp.where(_lane_group_mask(first.shape, HEAD_DIM, 0), first, second)


def _diff_attn_kernel(q_ref, k_ref, v_ref, bias_ref, lam_ref, g_ref, o_ref, m_ref, l_ref, acc_ref):
    qi = pl.program_id(2)
    q = q_ref[...]
    n_maps = LANES // A_QK_DIM
    qm = [jnp.where(_lane_group_mask(q.shape, A_QK_DIM, j), q, jnp.zeros_like(q)) for j in range(n_maps)]
    _init_softmax(m_ref, l_ref, acc_ref)

    def block(j, bias_off):
        k, v = _kv_block(k_ref, v_ref, j)
        for mp in range(n_maps):
            s = _nt_dot(qm[mp], k)
            if bias_off is not None:
                s = s + bias_ref[mp // 2, bias_off]
            _softmax_step(s, v, m_ref.at[mp], l_ref.at[mp], acc_ref.at[mp])

    block(qi, 0)

    @pl.when(qi >= 1)
    def _():
        block(qi - 1, 1)

    def far(j, carry):
        block(j, None)
        return carry

    lax.fori_loop(0, jnp.maximum(qi - 1, 0), far, 0)

    lam = lam_ref[...]
    heads = []
    for h in range(HEADS_PER_TILE):
        o0 = acc_ref[2 * h] / l_ref[2 * h]
        o1 = acc_ref[2 * h + 1] / l_ref[2 * h + 1]
        heads.append(o0 - lam * o1)
    o = _pair_select(heads[0], heads[1])
    first = _lane_group_mask(o.shape, HEAD_DIM, 0)
    sq = o * o
    ss0 = jnp.sum(jnp.where(first, sq, 0.0), axis=-1, keepdims=True)
    ss1 = jnp.sum(jnp.where(first, 0.0, sq), axis=-1, keepdims=True)
    ms = jnp.where(first, ss0, ss1) * (1.0 / HEAD_DIM)
    o_ref[...] = (o * lax.rsqrt(ms + LN_EPS) * g_ref[...]).astype(o_ref.dtype)


def _diff_attn(p, bias_tiles, lam_row, g_row, q_col, k_col, v_col):
    b, s, _ = p.shape
    q_spec, k_spec, v_spec, o_spec = _attn_specs(q_col, k_col, v_col, s)
    bias_spec = pl.BlockSpec((HEADS_PER_TILE, 2, TQ, TK), lambda b, hp, qi: (hp, 0, 0, 0))
    row_spec = pl.BlockSpec((1, LANES), lambda b, hp, qi: (0, 0))
    return pl.pallas_call(
        _diff_attn_kernel, out_shape=_attn_out_shape(p),
        grid=(b, N_PAIRS, s // TQ),
        in_specs=[q_spec, k_spec, v_spec, bias_spec, row_spec, row_spec],
        out_specs=o_spec, scratch_shapes=_softmax_scratch(LANES // A_QK_DIM),
        compiler_params=_params("parallel", "parallel", "arbitrary"), name="diff_attn",
    )(p, p, p, bias_tiles, lam_row, g_row)


def _sb_attn_kernel(q_ref, k_ref, v_ref, o_ref, c_ref, acc_ref):
    qi = pl.program_id(2)
    q = q_ref[...]
    row = lax.broadcasted_iota(jnp.int32, (TQ, TK), 0)
    col = lax.broadcasted_iota(jnp.int32, (TQ, TK), 1)
    past = col < row
    later = (row > col).astype(BF16)
    heads = []
    for h in range(HEADS_PER_TILE):
        qh = jnp.where(_lane_group_mask(q.shape, HEAD_DIM, h), q, jnp.zeros_like(q))
        c_ref[...] = jnp.zeros(c_ref.shape, F32)
        acc_ref[...] = jnp.zeros(acc_ref.shape, F32)

        def block(j, diag, qh=qh):
            k, v = _kv_block(k_ref, v_ref, j)
            z = _nt_dot(qh, k)
            log_beta, log_rem = _log_sigmoid_pair(z)
            if diag:
                log_rem = jnp.where(past, log_rem, 0.0)
            hi = log_rem.astype(BF16)
            lo = (log_rem - hi.astype(F32)).astype(BF16)
            after = _dot(hi, later) + _dot(lo, later) + c_ref[...]
            w = jnp.exp(log_beta + after)
            if diag:
                w = jnp.where(past, w, 0.0)
            acc_ref[...] += _dot(w.astype(BF16), v)
            c_new = c_ref[...] + jnp.sum(log_rem, axis=-1, keepdims=True)
            c_ref[...] = c_new
            return jnp.max(c_new)

        c_max = block(qi, True)

        def cond(state):
            return jnp.logical_and(state[0] >= 0, state[1] > SB_CUTOFF)

        def body(state, block=block):
            return state[0] - 1, block(state[0], False)

        lax.while_loop(cond, body, (qi - 1, c_max))
        heads.append(acc_ref[...])
    o_ref[...] = _pair_select(heads[0], heads[1]).astype(o_ref.dtype)


def _sb_attn(p, q_col, k_col, v_col):
    b, s, _ = p.shape
    q_spec, k_spec, v_spec, o_spec = _attn_specs(q_col, k_col, v_col, s)
    return pl.pallas_call(
        _sb_attn_kernel, out_shape=_attn_out_shape(p),
        grid=(b, N_PAIRS, s // TQ),
        in_specs=[q_spec, k_spec, v_spec], out_specs=o_spec,
        scratch_shapes=[pltpu.VMEM((TQ, 1), F32), pltpu.VMEM((TQ, LANES), F32)],
        compiler_params=_params("parallel", "parallel", "arbitrary"), name="sb_attn",
    )(p, p, p)


N_DIL_OFFSETS = max(w for w, _ in DILATED_PATTERNS) // TK + 1


def _dil_attn_kernel(q_ref, k_ref, v_ref, near_ref, far_ref, o_ref, m_ref, l_ref, acc_ref):
    qi = pl.program_id(2)
    q = q_ref[...]
    qh = [jnp.where(_lane_group_mask(q.shape, HEAD_DIM, h), q, jnp.zeros_like(q))
          for h in range(HEADS_PER_TILE)]
    _init_softmax(m_ref, l_ref, acc_ref)

    def block(j, tiles):
        k, v = _kv_block(k_ref, v_ref, j)
        for h in range(HEADS_PER_TILE):
            s = _nt_dot(qh[h], k) + tiles[h]
            _softmax_step(s, v, m_ref.at[h], l_ref.at[h], acc_ref.at[h])

    block(qi, [near_ref[h, 0] for h in range(HEADS_PER_TILE)])

    @pl.when(qi >= 1)
    def _():
        block(qi - 1, [near_ref[h, 1] for h in range(HEADS_PER_TILE)])

    def far(off, carry):
        tile = far_ref[off]
        block(qi - off, [tile] * HEADS_PER_TILE)
        return carry

    lax.fori_loop(2, jnp.minimum(qi, N_DIL_OFFSETS - 1) + 1, far, 0)
    o_ref[...] = _pair_select(acc_ref[0] / l_ref[0], acc_ref[1] / l_ref[1]).astype(o_ref.dtype)


def _dil_attn(p, near_tiles, far_tiles, q_col, k_col, v_col):
    b, s, _ = p.shape
    q_spec, k_spec, v_spec, o_spec = _attn_specs(q_col, k_col, v_col, s)
    near_spec = pl.BlockSpec((HEADS_PER_TILE, 2, TQ, TK), lambda b, hp, qi: (hp, 0, 0, 0))
    far_spec = pl.BlockSpec((N_DIL_OFFSETS, TQ, TK), lambda b, hp, qi: (0, 0, 0))
    return pl.pallas_call(
        _dil_attn_kernel, out_shape=_attn_out_shape(p),
        grid=(b, N_PAIRS, s // TQ),
        in_specs=[q_spec, k_spec, v_spec, near_spec, far_spec],
        out_specs=o_spec, scratch_shapes=_softmax_scratch(HEADS_PER_TILE),
        compiler_params=_params("parallel", "parallel", "arbitrary"), name="dil_attn",
    )(p, p, p, near_tiles, far_tiles)


def _fox_gate_kernel(fg_ref, bf_ref, f_ref):
    row = lax.broadcasted_iota(jnp.int32, (LANES, LANES), 0)
    col = lax.broadcasted_iota(jnp.int32, (LANES, LANES), 1)
    upto = (row <= col).astype(F32)
    carry = jnp.zeros((fg_ref.shape[0], 1), F32)
    for blk in range(fg_ref.shape[1] // LANES):
        sl = slice(blk * LANES, (blk + 1) * LANES)
        log_f, _ = _log_sigmoid_pair(fg_ref[:, sl] + bf_ref[...])
        cs = jnp.dot(log_f, upto, preferred_element_type=F32, precision=lax.Precision.HIGHEST) + carry
        f_ref[:, sl] = cs
        carry = cs[:, LANES - 1:LANES]


def _fox_gates(fg_t, b_f, seq):
    heads, n = fg_t.shape
    return pl.pallas_call(
        _fox_gate_kernel, out_shape=jax.ShapeDtypeStruct((heads, n), F32),
        grid=(n // seq,),
        in_specs=[pl.BlockSpec((heads, seq), lambda b: (0, b)), pl.BlockSpec((heads, 1), lambda b: (0, 0))],
        out_specs=pl.BlockSpec((heads, seq), lambda b: (0, b)),
        compiler_params=_params("parallel"), name="fox_gates",
    )(fg_t, b_f.reshape(heads, 1))


def _fox_attn_kernel(q_ref, k_ref, v_ref, f_ref, o_ref, m_ref, l_ref, acc_ref):
    qi = pl.program_id(2)
    q = q_ref[...]
    qh = [jnp.where(_lane_group_mask(q.shape, HEAD_DIM, h), q, jnp.zeros_like(q))
          for h in range(HEADS_PER_TILE)]
    _init_softmax(m_ref, l_ref, acc_ref)
    row = lax.broadcasted_iota(jnp.int32, (TQ, TK), 0)
    col = lax.broadcasted_iota(jnp.int32, (TQ, TK), 1)
    causal = col <= row

    def block(j, diag):
        k, v = _kv_block(k_ref, v_ref, j)
        start = pl.multiple_of(j * TK, TK)
        for h in range(HEADS_PER_TILE):
            s = _nt_dot(qh[h], k) - f_ref[h:h + 1, pl.ds(start, TK)]
            if diag:
                s = jnp.where(causal, s, NEG)
            _softmax_step(s, v, m_ref.at[h], l_ref.at[h], acc_ref.at[h])

    block(qi, True)

    def far(j, carry):
        block(j, False)
        return carry

    lax.fori_loop(0, qi, far, 0)
    o_ref[...] = _pair_select(acc_ref[0] / l_ref[0], acc_ref[1] / l_ref[1]).astype(o_ref.dtype)


def _fox_attn(p, f_pairs, q_col, k_col, v_col):
    b, s, _ = p.shape
    q_spec, k_spec, v_spec, o_spec = _attn_specs(q_col, k_col, v_col, s)
    f_spec = pl.BlockSpec((None, HEADS_PER_TILE, s), lambda b, hp, qi: (hp, 0, b))
    return pl.pallas_call(
        _fox_attn_kernel, out_shape=_attn_out_shape(p),
        grid=(b, N_PAIRS, s // TQ),
        in_specs=[q_spec, k_spec, v_spec, f_spec],
        out_specs=o_spec, scratch_shapes=_softmax_scratch(HEADS_PER_TILE),
        compiler_params=_params("parallel", "parallel", "arbitrary"), name="fox_attn",
    )(p, p, p, f_pairs)


def _top_values(work, out_ref, count):
    for r in range(count):
        m = jnp.max(work, axis=0, keepdims=True)
        out_ref[r:r + 1, :] = m
        work = jnp.where(work == m, -jnp.inf, work)


def _peer_select_kernel(q_ref, keys_ref, s1_ref, s2_ref, e1_ref, e2_ref, tau_ref, a_ref, b_ref, sel_ref):
    k = PEER_TOPK
    sub = 8
    row_sub = lax.broadcasted_iota(jnp.int32, (sub, q_ref.shape[0]), 0)
    for h in range(PEER_HEADS):
        s = [_nt_dot(keys_ref[2 * h + c], q_ref[:, (2 * h + c) * LANES:(2 * h + c + 1) * LANES])
             for c in range(2)]
        _top_values(s[0], a_ref, k)
        _top_values(s[1], b_ref, k)
        a = a_ref[...]
        b = b_ref[...]
        groups = [a[0:1] + b]
        for r1 in range(1, sub):
            g = a[r1:r1 + 1] + b[0:sub]
            groups.append(jnp.where(row_sub < k // (r1 + 1), g, -jnp.inf))
        groups.append(a[sub:k] + b[0:1])
        _top_values(jnp.concatenate(groups, axis=0), sel_ref, k)
        sel = sel_ref[...]
        z = jnp.sum(jnp.exp(sel - sel[0:1]), axis=0, keepdims=True)
        s1_ref[h] = s[0]
        s2_ref[h] = s[1]
        e1_ref[h] = jnp.exp(s[0] - a[0:1]) / z
        e2_ref[h] = jnp.exp(s[1] - b[0:1])
        tau_ref[h:h + 1, :] = sel[k - 1:k]


def _peer_select(q, keys):
    n = q.shape[0]
    t = PEER_SEL_TOKENS
    hk = jax.ShapeDtypeStruct((PEER_HEADS, PEER_NKEYS, n), F32)
    hk_spec = pl.BlockSpec((PEER_HEADS, PEER_NKEYS, t), lambda i: (0, 0, i))
    return pl.pallas_call(
        _peer_select_kernel,
        out_shape=(hk, hk, hk, hk, jax.ShapeDtypeStruct((PEER_HEADS, n), F32)),
        grid=(n // t,),
        in_specs=[pl.BlockSpec((t, q.shape[1]), lambda i: (i, 0)),
                  pl.BlockSpec(keys.shape, lambda i: (0, 0, 0))],
        out_specs=(hk_spec, hk_spec, hk_spec, hk_spec, pl.BlockSpec((PEER_HEADS, t), lambda i: (0, i))),
        scratch_shapes=[pltpu.VMEM((PEER_TOPK, t), F32)] * 3,
        compiler_params=_params("parallel"), name="peer_select",
    )(q, keys)


def _gelu(a):
    return 0.5 * a * (1.0 + lax.erf(a * (1.0 / math.sqrt(2.0))))


def _peer_ffn_kernel(h_ref, u_ref, vt_ref, s1_ref, e1_ref, s2_ref, e2_ref, tau_ref, g_ref, b_ref,
                     o_ref, acc_ref, act_ref, w_ref, *, alpha):
    ec = pl.program_id(1)

    @pl.when(ec == 0)
    def _():
        acc_ref[...] = jnp.zeros(acc_ref.shape, F32)

    x = h_ref[...].astype(BF16)
    act_ref[...] = _nt_dot(u_ref[...], x)

    def rows(ii, carry):
        sl = pl.ds(pl.multiple_of(ii * PEER_NKEYS, PEER_NKEYS), PEER_NKEYS)
        gate = jnp.zeros((PEER_NKEYS, x.shape[0]), F32)
        for h in range(PEER_HEADS):
            pair_sum = s2_ref[h] + s1_ref[h, pl.ds(ii, 1), :]
            pair_gate = e2_ref[h] * e1_ref[h, pl.ds(ii, 1), :]
            gate = gate + jnp.where(pair_sum >= tau_ref[h:h + 1, :], pair_gate, 0.0)
        w_ref[sl, :] = (gate * _gelu(act_ref[sl, :])).astype(BF16)
        return carry

    lax.fori_loop(0, PEER_ROWS_PER_CHUNK, rows, 0)
    acc_ref[...] += _dot(vt_ref[...], w_ref[...])

    @pl.when(ec == pl.num_programs(1) - 1)
    def _():
        ffn = acc_ref[...].T
        o_ref[...] = _layer_norm(alpha * h_ref[...] + ffn, g_ref[...], b_ref[...])


def _peer_ffn(h, u, v_t, s1, s2, e1, e2, tau, g, b, alpha):
    n, d = h.shape
    t = PEER_TOKENS
    rows = PEER_ROWS_PER_CHUNK
    ec_size = rows * PEER_NKEYS
    n_chunks = u.shape[0] // ec_size
    tok = pl.BlockSpec((t, d), lambda i, e: (i, 0))
    chunk_rows = pl.BlockSpec((PEER_HEADS, rows, t), lambda i, e: (0, e, i))
    all_rows = pl.BlockSpec((PEER_HEADS, PEER_NKEYS, t), lambda i, e: (0, 0, i))
    vec = pl.BlockSpec((1, d), lambda i, e: (0, 0))
    return pl.pallas_call(
        functools.partial(_peer_ffn_kernel, alpha=alpha),
        out_shape=jax.ShapeDtypeStruct((n, d), F32),
        grid=(n // t, n_chunks),
        in_specs=[tok,
                  pl.BlockSpec((ec_size, d), lambda i, e: (e, 0)),
                  pl.BlockSpec((d, ec_size), lambda i, e: (0, e)),
                  chunk_rows, chunk_rows, all_rows, all_rows,
                  pl.BlockSpec((PEER_HEADS, t), lambda i, e: (0, i)),
                  vec, vec],
        out_specs=tok,
        scratch_shapes=[pltpu.VMEM((d, t), F32), pltpu.VMEM((ec_size, t), F32), pltpu.VMEM((ec_size, t), BF16)],
        compiler_params=_params("parallel", "arbitrary"), name="peer_ffn",
    )(h, u, v_t, s1, e1, s2, e2, tau, g.reshape(1, d), b.reshape(1, d))


def _peer(h, w_q, sub_keys, u, v, g, b, alpha):
    q = _project(h, w_q.astype(BF16))
    keys = sub_keys.reshape(PEER_HEADS * 2, PEER_NKEYS, -1).astype(BF16)
    s1, s2, e1, e2, tau = _peer_select(q, keys)
    return _peer_ffn(h, u.astype(BF16), v.T.astype(BF16), s1, s2, e1, e2, tau, g, b, alpha)


def _t5_bucket(dist):
    max_exact = T5_BUCKETS // 2
    n = jnp.maximum(dist, 0)
    nf = jnp.maximum(n, 1).astype(F32)
    large = max_exact + (jnp.log(nf / max_exact) / math.log(T5_MAX_DIST / max_exact)
                         * (T5_BUCKETS - max_exact)).astype(jnp.int32)
    large = jnp.minimum(large, T5_BUCKETS - 1)
    return jnp.where(n < max_exact, n, large)


def _t5_near_tiles(t5_bias):
    i = jnp.arange(TQ)[:, None]
    j = jnp.arange(TK)[None, :]
    tiles = []
    for off in range(2):
        dist = off * TK + i - j
        bias = t5_bias[_t5_bucket(dist)] - t5_bias[T5_BUCKETS - 1]
        bias = jnp.where((dist >= 0)[..., None], bias, NEG)
        tiles.append(jnp.moveaxis(bias, -1, 0))
    return jnp.stack(tiles, axis=1).astype(F32)


def _dilated_log_multiplicity():
    i = np.arange(TQ)[:, None]
    j = np.arange(TK)[None, :]
    tiles = []
    for off in range(N_DIL_OFFSETS):
        d = off * TK + i - j
        count = np.zeros(d.shape, np.float64)
        for window, dil in DILATED_PATTERNS:
            count += (d >= 0) & (d <= window) & (d % dil == 0)
        tiles.append(np.where(count > 0, np.log(np.maximum(count, 1.0)), NEG))
    return np.stack(tiles).astype(np.float32)


def _scale_columns(w, col_scales):
    scale = np.ones((w.shape[1],), np.float32)
    for start, stop, value in col_scales:
        scale[start:stop] = value
    return w * scale


def kernel(x, t5_bias, ab_w_in, ab_lambda, ab_subln_g, ab_w_out, cd_w_in, cd_b_f, cd_w_out,
           peer_w_q, peer_sub_keys, peer_u, peer_v, ln_g, ln_b):
    bsz, seq, d = x.shape
    depth = ln_g.shape[0]
    alpha = (2 * depth) ** 0.25
    n = bsz * seq
    gw = N_HEADS * HEAD_DIM
    blk = lambda g: g * gw // LANES
    h = x.reshape(n, d)
    t5_tiles = _t5_near_tiles(t5_bias)
    log_mult = _dilated_log_multiplicity()
    dil_near = t5_tiles + log_mult[None, :2]
    dil_far = jnp.asarray(log_mult)
    for layer in range(depth):
        j = layer // 2
        if layer % 2 == 0:
            lambda_init = 0.8 - 0.6 * math.exp(-0.3 * layer)
            w_in = _scale_columns(ab_w_in[j], [(0, gw, A_QK_DIM ** -0.5), (3 * gw, 4 * gw, HEAD_DIM ** -0.5)])
            p = _project(h, w_in.astype(BF16)).reshape(bsz, seq, -1)
            lv = ab_lambda[j].astype(F32)
            lam = jnp.exp(jnp.sum(lv[0] * lv[1])) - jnp.exp(jnp.sum(lv[2] * lv[3])) + lambda_init
            lam_row = jnp.full((1, LANES), lam, F32)
            g_row = jnp.tile(ab_subln_g[j].astype(F32) * (1.0 - lambda_init), HEADS_PER_TILE).reshape(1, LANES)
            o1 = _diff_attn(p, t5_tiles, lam_row, g_row, blk(0), blk(1), blk(2))
            o2 = _sb_attn(p, blk(3), blk(4), blk(5))
            w_out = ab_w_out[j]
        else:
            w_full = _scale_columns(cd_w_in[j], [(0, gw, HEAD_DIM ** -0.5), (3 * gw, 4 * gw, HEAD_DIM ** -0.5)])
            p, fg_t = _project(h, w_full[:, :6 * gw].astype(BF16), w_full[:, 6 * gw:].T.astype(BF16))
            p = p.reshape(bsz, seq, -1)
            f_cum = _fox_gates(fg_t, cd_b_f[j].astype(F32), seq)
            o1 = _dil_attn(p, dil_near, dil_far, blk(0), blk(1), blk(2))
            o2 = _fox_attn(p, f_cum.reshape(N_PAIRS, HEADS_PER_TILE, n), blk(3), blk(4), blk(5))
            w_out = cd_w_out[j]
        h = _mix_ln(o1.reshape(n, gw), o2.reshape(n, gw), w_out.astype(BF16), h,
                    ln_g[layer, 0], ln_b[layer, 0], alpha)
        h = _peer(h, peer_w_q[layer], peer_sub_keys[layer], peer_u[layer], peer_v[layer],
                  ln_g[layer, 1], ln_b[layer, 1], alpha)
    return h.reshape(bsz, seq, d)
```

```python
import functools
import math

import numpy as np
import jax
import jax.numpy as jnp
from jax import lax
from jax.experimental import pallas as pl
from jax.experimental.pallas import tpu as pltpu

F32 = jnp.float32
BF16 = jnp.bfloat16

HEAD_DIM = 64
N_HEADS = 8
A_QK_DIM = 32
T5_BUCKETS = 32
T5_MAX_DIST = 128
DILATED_PATTERNS = ((128, 1), (512, 4), (2048, 16))
PEER_HEADS = 8
PEER_NKEYS = 128
PEER_TOPK = 16
LN_EPS = 1e-5

LANES = 128
HEADS_PER_TILE = LANES // HEAD_DIM
N_PAIRS = N_HEADS // HEADS_PER_TILE
TQ = 256
TK = 256
ROW_TILE = 512
GATE_TOKENS = 512
PEER_SEL_TOKENS = 256
PEER_TOKENS = 512
PEER_ROWS_PER_CHUNK = 8
VMEM_LIMIT_BYTES = 56 * 1024 * 1024
NEG = -1e30
SB_CUTOFF = -110.0


def _nt_dot(a, b):
    return lax.dot_general(a, b, (((1,), (1,)), ((), ())), preferred_element_type=F32)


def _dot(a, b):
    return jnp.dot(a, b, preferred_element_type=F32)


def _params(*sem):
    return pltpu.CompilerParams(dimension_semantics=sem, vmem_limit_bytes=VMEM_LIMIT_BYTES)


def _layer_norm(y, g, b):
    mu = jnp.mean(y, axis=-1, keepdims=True)
    yc = y - mu
    var = jnp.mean(yc * yc, axis=-1, keepdims=True)
    return yc * lax.rsqrt(var + LN_EPS) * g + b


def _log_sigmoid_pair(z):
    sp = jnp.log(1.0 + jnp.exp(-jnp.abs(z)))
    return jnp.minimum(z, 0.0) - sp, -jnp.maximum(z, 0.0) - sp


def _proj_kernel(x_ref, w_ref, o_ref, *, tn):
    x = x_ref[...].astype(BF16)
    for c in range(w_ref.shape[1] // tn):
        o_ref[:, c * tn:(c + 1) * tn] = _dot(x, w_ref[:, c * tn:(c + 1) * tn]).astype(o_ref.dtype)


def _project(x, w, tn=512):
    n, k = x.shape
    m = w.shape[1]
    tm = ROW_TILE
    return pl.pallas_call(
        functools.partial(_proj_kernel, tn=tn),
        out_shape=jax.ShapeDtypeStruct((n, m), BF16),
        grid=(n // tm,),
        in_specs=[pl.BlockSpec((tm, k), lambda i: (i, 0)), pl.BlockSpec((k, m), lambda i: (0, 0))],
        out_specs=pl.BlockSpec((tm, m), lambda i: (i, 0)),
        compiler_params=_params("parallel"), name="proj")(x, w)


def _proj_attn_kernel(x_ref, wt_ref, wf_ref, *rest, tn, has_gate):
    if has_gate:
        wg_ref, tok_ref, feat_ref, gate_ref = rest
    else:
        tok_ref, feat_ref = rest
    x = x_ref[...].astype(BF16)
    for c in range(wt_ref.shape[1] // tn):
        sl = slice(c * tn, (c + 1) * tn)
        tok_ref[:, sl] = _dot(x, wt_ref[:, sl]).astype(BF16)
    for c in range(wf_ref.shape[0] // tn):
        sl = slice(c * tn, (c + 1) * tn)
        feat_ref[sl, :] = _nt_dot(wf_ref[sl, :], x).astype(BF16)
    if has_gate:
        gate_ref[...] = _dot(x, wg_ref[...])


def _project_attn(x, w_tok, w_feat_t, w_gate=None, tn=512):
    n, k = x.shape
    mt, mf = w_tok.shape[1], w_feat_t.shape[0]
    tm = ROW_TILE
    has_gate = w_gate is not None
    in_specs = [pl.BlockSpec((tm, k), lambda i: (i, 0)),
                pl.BlockSpec((k, mt), lambda i: (0, 0)),
                pl.BlockSpec((mf, k), lambda i: (0, 0))]
    out_shape = [jax.ShapeDtypeStruct((n, mt), BF16), jax.ShapeDtypeStruct((mf, n), BF16)]
    out_specs = [pl.BlockSpec((tm, mt), lambda i: (i, 0)), pl.BlockSpec((mf, tm), lambda i: (0, i))]
    args = [x, w_tok, w_feat_t]
    if has_gate:
        in_specs.append(pl.BlockSpec((k, LANES), lambda i: (0, 0)))
        out_shape.append(jax.ShapeDtypeStruct((n, LANES), F32))
        out_specs.append(pl.BlockSpec((tm, LANES), lambda i: (i, 0)))
        args.append(w_gate)
    return pl.pallas_call(
        functools.partial(_proj_attn_kernel, tn=tn, has_gate=has_gate),
        out_shape=tuple(out_shape), grid=(n // tm,), in_specs=in_specs, out_specs=tuple(out_specs),
        compiler_params=_params("parallel"), name="proj_attn")(*args)


def _mix_ln_kernel(oa_ref, ob_ref, wa_ref, wb_ref, h_ref, g_ref, b_ref, o_ref, *, alpha):
    mix = _dot(oa_ref[...], wa_ref[...]) + _dot(ob_ref[...], wb_ref[...])
    o_ref[...] = _layer_norm(alpha * h_ref[...] + mix, g_ref[...], b_ref[...])


def _mix_ln(oa, ob, w_out, h, g, b, alpha):
    n, d = h.shape
    wa_rows = oa.shape[1]
    tm = ROW_TILE
    row = lambda width: pl.BlockSpec((tm, width), lambda i: (i, 0))
    full = lambda r, c: pl.BlockSpec((r, c), lambda i: (0, 0))
    return pl.pallas_call(
        functools.partial(_mix_ln_kernel, alpha=alpha),
        out_shape=jax.ShapeDtypeStruct((n, d), F32),
        grid=(n // tm,),
        in_specs=[row(wa_rows), row(ob.shape[1]), full(wa_rows, d), full(ob.shape[1], d),
                  row(d), full(1, d), full(1, d)],
        out_specs=row(d),
        compiler_params=_params("parallel"), name="mix_ln",
    )(oa, ob, w_out[:wa_rows], w_out[wa_rows:], h, g.reshape(1, d), b.reshape(1, d))


def _row_group_mask(shape, width, index):
    row = lax.broadcasted_iota(jnp.int32, shape, 0)
    return (row // width) == index


def _masked_queries(qt_ref, width):
    qt = qt_ref[...].astype(F32)
    return [jnp.where(_row_group_mask(qt.shape, width, j), qt, 0.0).astype(BF16)
            for j in range(LANES // width)]


def _score_fn(k_ref, qm, s_ref):
    def qk(j, slot):
        k = k_ref[pl.ds(pl.multiple_of(j * TK, TK), TK), :]
        for mp, q in enumerate(qm):
            s_ref[slot, mp] = _dot(k, q)
    return qk


def _flash_block(i, first_block, n_blocks, qk, s_ref, vt_ref, heads_of_map, adjust, m_ref, l_ref, acc_ref):
    slot = i % 2
    qk(first_block - jnp.minimum(i + 1, n_blocks - 1), 1 - slot)
    start = pl.multiple_of((first_block - i) * TK, TK)
    vt = vt_ref[:, pl.ds(start, TK)]
    for mp, h in enumerate(heads_of_map):
        st = adjust(mp, s_ref[slot, mp], start)
        m_old = m_ref[mp]
        m_new = jnp.maximum(m_old, jnp.max(st, axis=0, keepdims=True))
        alpha = jnp.exp(m_old - m_new)
        p = jnp.exp(st - m_new)
        l_ref[mp] = alpha * l_ref[mp] + jnp.sum(p, axis=0, keepdims=True)
        pv = _dot(vt[h * HEAD_DIM:(h + 1) * HEAD_DIM, :], p.astype(BF16))
        acc_ref[mp] = alpha * acc_ref[mp] + pv
        m_ref[mp] = m_new


def _attn_specs(q_row, k_col, v_row, seq):
    nq = seq // TQ
    q_spec = pl.BlockSpec((LANES, TQ), lambda b, hp, qi: (q_row + hp, b * nq + qi))
    k_spec = pl.BlockSpec((None, seq, LANES), lambda b, hp, qi: (b, 0, k_col + hp))
    v_spec = pl.BlockSpec((LANES, seq), lambda b, hp, qi: (v_row + hp, b))
    o_spec = pl.BlockSpec((None, TQ, LANES), lambda b, hp, qi: (b, qi, hp))
    return q_spec, k_spec, v_spec, o_spec


def _attn_out_shape(tok):
    return jax.ShapeDtypeStruct((tok.shape[0], tok.shape[1], N_HEADS * HEAD_DIM), BF16)


def _softmax_scratch(n):
    return [pltpu.VMEM((2, n, TK, TQ), F32), pltpu.VMEM((n, 1, TQ), F32), pltpu.VMEM((n, 1, TQ), F32),
            pltpu.VMEM((n, HEAD_DIM, TQ), F32)]


def _init_softmax(m_ref, l_ref, acc_ref):
    m_ref[...] = jnp.full(m_ref.shape, NEG, F32)
    l_ref[...] = jnp.zeros(l_ref.shape, F32)
    acc_ref[...] = jnp.zeros(acc_ref.shape, F32)


def _store_heads(o_ref, head_rows):
    o_ref[...] = jnp.concatenate(head_rows, axis=0).T.astype(o_ref.dtype)


_ATTN_SEM = ("parallel", "parallel", "arbitrary")


def _diff_attn_kernel(qt_ref, k_ref, vt_ref, bias_ref, lam_ref, g_ref, o_ref, s_ref, m_ref, l_ref, acc_ref):
    qi = pl.program_id(2)
    qk = _score_fn(k_ref, _masked_queries(qt_ref, A_QK_DIM), s_ref)
    heads_of_map = (0, 0, 1, 1)
    _init_softmax(m_ref, l_ref, acc_ref)
    n_blocks = qi + 1
    qk(qi, 0)

    def block(i, adjust):
        _flash_block(i, qi, n_blocks, qk, s_ref, vt_ref, heads_of_map, adjust, m_ref, l_ref, acc_ref)

    block(0, lambda mp, st, start: st + bias_ref[mp // 2, 0])

    @pl.when(qi >= 1)
    def _():
        block(1, lambda mp, st, start: st + bias_ref[mp // 2, 1])

    def far(i, carry):
        block(i, lambda mp, st, start: st)
        return carry

    lax.fori_loop(2, n_blocks, far, 0)

    lam = lam_ref[...]
    heads = []
    for h in range(HEADS_PER_TILE):
        o = acc_ref[2 * h] / l_ref[2 * h] - lam * (acc_ref[2 * h + 1] / l_ref[2 * h + 1])
        ms = jnp.mean(o * o, axis=0, keepdims=True)
        heads.append(o * lax.rsqrt(ms + LN_EPS) * g_ref[...])
    _store_heads(o_ref, heads)


def _diff_attn(tok, feat, bias_tiles, lam_row, g_tile, q_row, k_col, v_row):
    b, s, _ = tok.shape
    q_spec, k_spec, v_spec, o_spec = _attn_specs(q_row, k_col, v_row, s)
    bias_spec = pl.BlockSpec((HEADS_PER_TILE, 2, TK, TQ), lambda b, hp, qi: (hp, 0, 0, 0))
    return pl.pallas_call(
        _diff_attn_kernel, out_shape=_attn_out_shape(tok),
        grid=(b, N_PAIRS, s // TQ),
        in_specs=[q_spec, k_spec, v_spec, bias_spec,
                  pl.BlockSpec((1, TQ), lambda b, hp, qi: (0, 0)),
                  pl.BlockSpec((HEAD_DIM, TQ), lambda b, hp, qi: (0, 0))],
        out_specs=o_spec, scratch_shapes=_softmax_scratch(LANES // A_QK_DIM),
        compiler_params=_params(*_ATTN_SEM), name="diff_attn",
    )(feat, tok, feat, bias_tiles, lam_row, g_tile)


def _sb_attn_kernel(qt_ref, k_ref, vt_ref, o_ref, s_ref, c_ref, acc_ref):
    qi = pl.program_id(2)
    qk = _score_fn(k_ref, _masked_queries(qt_ref, HEAD_DIM), s_ref)
    key = lax.broadcasted_iota(jnp.int32, (TK, TQ), 0)
    qry = lax.broadcasted_iota(jnp.int32, (TK, TQ), 1)
    past = key < qry
    later = (qry > key).astype(BF16)
    c_ref[...] = jnp.zeros(c_ref.shape, F32)
    acc_ref[...] = jnp.zeros(acc_ref.shape, F32)
    qk(qi, 0)

    def block(i, diag):
        slot = i % 2
        j = qi - i
        qk(jnp.maximum(j - 1, 0), 1 - slot)
        vt = vt_ref[:, pl.ds(pl.multiple_of(j * TK, TK), TK)]
        c_max = []
        for h in range(HEADS_PER_TILE):
            log_beta, log_rem = _log_sigmoid_pair(s_ref[slot, h])
            if diag:
                log_rem = jnp.where(past, log_rem, 0.0)
            hi = log_rem.astype(BF16)
            lo = (log_rem - hi.astype(F32)).astype(BF16)
            after = _dot(later, hi) + _dot(later, lo) + c_ref[h]
            w = jnp.exp(log_beta + after)
            if diag:
                w = jnp.where(past, w, 0.0)
            acc_ref[h] += _dot(vt[h * HEAD_DIM:(h + 1) * HEAD_DIM, :], w.astype(BF16))
            c_new = c_ref[h] + jnp.sum(log_rem, axis=0, keepdims=True)
            c_ref[h] = c_new
            c_max.append(jnp.max(c_new))
        return jnp.maximum(c_max[0], c_max[1])

    first = block(0, True)

    def cond(state):
        return jnp.logical_and(state[0] <= qi, state[1] > SB_CUTOFF)

    def body(state):
        return state[0] + 1, block(state[0], False)

    lax.while_loop(cond, body, (jnp.int32(1), first))
    _store_heads(o_ref, [acc_ref[h] for h in range(HEADS_PER_TILE)])


def _sb_attn(tok, feat, q_row, k_col, v_row):
    b, s, _ = tok.shape
    q_spec, k_spec, v_spec, o_spec = _attn_specs(q_row, k_col, v_row, s)
    return pl.pallas_call(
        _sb_attn_kernel, out_shape=_attn_out_shape(tok),
        grid=(b, N_PAIRS, s // TQ),
        in_specs=[q_spec, k_spec, v_spec], out_specs=o_spec,
        scratch_shapes=[pltpu.VMEM((2, HEADS_PER_TILE, TK, TQ), F32), pltpu.VMEM((HEADS_PER_TILE, 1, TQ), F32),
                        pltpu.VMEM((HEADS_PER_TILE, HEAD_DIM, TQ), F32)],
        compiler_params=_params(*_ATTN_SEM), name="sb_attn",
    )(feat, tok, feat)


N_DIL_OFFSETS = max(w for w, _ in DILATED_PATTERNS) // TK + 1


def _dil_attn_kernel(qt_ref, k_ref, vt_ref, tile_ref, o_ref, s_ref, m_ref, l_ref, acc_ref):
    qi = pl.program_id(2)
    qk = _score_fn(k_ref, _masked_queries(qt_ref, HEAD_DIM), s_ref)
    _init_softmax(m_ref, l_ref, acc_ref)
    n_blocks = jnp.minimum(qi, N_DIL_OFFSETS - 1) + 1
    qk(qi, 0)

    def body(i, carry):
        _flash_block(i, qi, n_blocks, qk, s_ref, vt_ref, (0, 1),
                     lambda mp, st, start: st + tile_ref[mp, i], m_ref, l_ref, acc_ref)
        return carry

    lax.fori_loop(0, n_blocks, body, 0)
    _store_heads(o_ref, [acc_ref[h] / l_ref[h] for h in range(HEADS_PER_TILE)])


def _dil_attn(tok, feat, tiles, q_row, k_col, v_row):
    b, s, _ = tok.shape
    q_spec, k_spec, v_spec, o_spec = _attn_specs(q_row, k_col, v_row, s)
    tile_spec = pl.BlockSpec((HEADS_PER_TILE, N_DIL_OFFSETS, TK, TQ), lambda b, hp, qi: (hp, 0, 0, 0))
    return pl.pallas_call(
        _dil_attn_kernel, out_shape=_attn_out_shape(tok),
        grid=(b, N_PAIRS, s // TQ),
        in_specs=[q_spec, k_spec, v_spec, tile_spec],
        out_specs=o_spec, scratch_shapes=_softmax_scratch(HEADS_PER_TILE),
        compiler_params=_params(*_ATTN_SEM), name="dil_attn",
    )(feat, tok, feat, tiles)


def _fox_gate_kernel(g_ref, bf_ref, f_ref, carry_ref):
    @pl.when(pl.program_id(1) == 0)
    def _():
        carry_ref[...] = jnp.zeros(carry_ref.shape, F32)

    row = lax.broadcasted_iota(jnp.int32, (LANES, LANES), 0)
    col = lax.broadcasted_iota(jnp.int32, (LANES, LANES), 1)
    upto = (col <= row).astype(F32)
    exact = functools.partial(jnp.dot, preferred_element_type=F32, precision=lax.Precision.HIGHEST)
    for blk in range(g_ref.shape[0] // LANES):
        sl = slice(blk * LANES, (blk + 1) * LANES)
        log_f, _ = _log_sigmoid_pair(g_ref[sl, :] + bf_ref[...])
        cs = exact(upto, log_f) + carry_ref[...]
        carry_ref[...] = cs[LANES - 1:LANES, :]
        for h in range(f_ref.shape[0]):
            f_ref[h, sl, :] = exact(cs, (row == h).astype(F32))


def _fox_gates(gate, b_f, bsz, seq):
    n = gate.shape[0]
    heads = b_f.shape[0]
    t = GATE_TOKENS
    nt = seq // t
    bf_row = jnp.zeros((1, LANES), F32).at[0, :heads].set(b_f)
    return pl.pallas_call(
        _fox_gate_kernel, out_shape=jax.ShapeDtypeStruct((heads, n, LANES), F32),
        grid=(bsz, nt),
        in_specs=[pl.BlockSpec((t, LANES), lambda b, i: (b * nt + i, 0)),
                  pl.BlockSpec((1, LANES), lambda b, i: (0, 0))],
        out_specs=pl.BlockSpec((heads, t, LANES), lambda b, i: (0, b * nt + i, 0)),
        scratch_shapes=[pltpu.VMEM((1, LANES), F32)],
        compiler_params=_params("parallel", "arbitrary"), name="fox_gates",
    )(gate, bf_row)


def _fox_attn_kernel(qt_ref, k_ref, vt_ref, f_ref, o_ref, s_ref, m_ref, l_ref, acc_ref):
    qi = pl.program_id(2)
    qk = _score_fn(k_ref, _masked_queries(qt_ref, HEAD_DIM), s_ref)
    _init_softmax(m_ref, l_ref, acc_ref)
    n_blocks = qi + 1
    key = lax.broadcasted_iota(jnp.int32, (TK, TQ), 0)
    qry = lax.broadcasted_iota(jnp.int32, (TK, TQ), 1)
    causal = key <= qry
    qk(qi, 0)

    def forget(mp, st, start):
        f = f_ref[mp, pl.ds(start, TK), :]
        return st - jnp.concatenate([f] * (TQ // LANES), axis=1)

    def block(i, adjust):
        _flash_block(i, qi, n_blocks, qk, s_ref, vt_ref, (0, 1), adjust, m_ref, l_ref, acc_ref)

    block(0, lambda mp, st, start: jnp.where(causal, forget(mp, st, start), NEG))

    def far(i, carry):
        block(i, forget)
        return carry

    lax.fori_loop(1, n_blocks, far, 0)
    _store_heads(o_ref, [acc_ref[h] / l_ref[h] for h in range(HEADS_PER_TILE)])


def _fox_attn(tok, feat, f_rep, q_row, k_col, v_row):
    b, s, _ = tok.shape
    q_spec, k_spec, v_spec, o_spec = _attn_specs(q_row, k_col, v_row, s)
    f_spec = pl.BlockSpec((HEADS_PER_TILE, s, LANES), lambda b, hp, qi: (hp, b, 0))
    return pl.pallas_call(
        _fox_attn_kernel, out_shape=_attn_out_shape(tok),
        grid=(b, N_PAIRS, s // TQ),
        in_specs=[q_spec, k_spec, v_spec, f_spec],
        out_specs=o_spec, scratch_shapes=_softmax_scratch(HEADS_PER_TILE),
        compiler_params=_params(*_ATTN_SEM), name="fox_attn",
    )(feat, tok, feat, f_rep)


def _top_values(work, out_ref, count):
    for r in range(count):
        m = jnp.max(work, axis=0, keepdims=True)
        out_ref[r:r + 1, :] = m
        work = jnp.where(work == m, -jnp.inf, work)


def _peer_select_kernel(q_ref, keys_ref, s1_ref, s2_ref, e1_ref, e2_ref, tau_ref, a_ref, b_ref, sel_ref):
    k = PEER_TOPK
    sub = 8
    row_sub = lax.broadcasted_iota(jnp.int32, (sub, q_ref.shape[0]), 0)
    for h in range(PEER_HEADS):
        s = [_nt_dot(keys_ref[2 * h + c], q_ref[:, (2 * h + c) * LANES:(2 * h + c + 1) * LANES])
             for c in range(2)]
        _top_values(s[0], a_ref, k)
        _top_values(s[1], b_ref, k)
        a = a_ref[...]
        b = b_ref[...]
        groups = [a[0:1] + b]
        for r1 in range(1, sub):
            g = a[r1:r1 + 1] + b[0:sub]
            groups.append(jnp.where(row_sub < k // (r1 + 1), g, -jnp.inf))
        groups.append(a[sub:k] + b[0:1])
        _top_values(jnp.concatenate(groups, axis=0), sel_ref, k)
        sel = sel_ref[...]
        z = jnp.sum(jnp.exp(sel - sel[0:1]), axis=0, keepdims=True)
        s1_ref[h] = s[0]
        s2_ref[h] = s[1]
        e1_ref[h] = jnp.exp(s[0] - a[0:1]) / z
        e2_ref[h] = jnp.exp(s[1] - b[0:1])
        tau_ref[h:h + 1, :] = sel[k - 1:k]


def _peer_select(q, keys):
    n = q.shape[0]
    t = PEER_SEL_TOKENS
    hk = jax.ShapeDtypeStruct((PEER_HEADS, PEER_NKEYS, n), F32)
    hk_spec = pl.BlockSpec((PEER_HEADS, PEER_NKEYS, t), lambda i: (0, 0, i))
    return pl.pallas_call(
        _peer_select_kernel,
        out_shape=(hk, hk, hk, hk, jax.ShapeDtypeStruct((PEER_HEADS, n), F32)),
        grid=(n // t,),
        in_specs=[pl.BlockSpec((t, q.shape[1]), lambda i: (i, 0)),
                  pl.BlockSpec(keys.shape, lambda i: (0, 0, 0))],
        out_specs=(hk_spec, hk_spec, hk_spec, hk_spec, pl.BlockSpec((PEER_HEADS, t), lambda i: (0, i))),
        scratch_shapes=[pltpu.VMEM((PEER_TOPK, t), F32)] * 3,
        compiler_params=_params("parallel"), name="peer_select",
    )(q, keys)


def _gelu(a):
    return 0.5 * a * (1.0 + lax.erf(a * (1.0 / math.sqrt(2.0))))


def _peer_ffn_kernel(h_ref, u_ref, vt_ref, s1_ref, e1_ref, s2_ref, e2_ref, tau_ref, g_ref, b_ref,
                     o_ref, acc_ref, act_ref, w_ref, *, alpha):
    ec = pl.program_id(1)

    @pl.when(ec == 0)
    def _():
        acc_ref[...] = jnp.zeros(acc_ref.shape, F32)

    x = h_ref[...].astype(BF16)
    act_ref[...] = _nt_dot(u_ref[...], x)

    def rows(ii, carry):
        sl = pl.ds(pl.multiple_of(ii * PEER_NKEYS, PEER_NKEYS), PEER_NKEYS)
        gate = jnp.zeros((PEER_NKEYS, x.shape[0]), F32)
        for h in range(PEER_HEADS):
            pair_sum = s2_ref[h] + s1_ref[h, pl.ds(ii, 1), :]
            pair_gate = e2_ref[h] * e1_ref[h, pl.ds(ii, 1), :]
            gate = gate + jnp.where(pair_sum >= tau_ref[h:h + 1, :], pair_gate, 0.0)
        w_ref[sl, :] = (gate * _gelu(act_ref[sl, :])).astype(BF16)
        return carry

    lax.fori_loop(0, PEER_ROWS_PER_CHUNK, rows, 0)
    acc_ref[...] += _dot(vt_ref[...], w_ref[...])

    @pl.when(ec == pl.num_programs(1) - 1)
    def _():
        ffn = acc_ref[...].T
        o_ref[...] = _layer_norm(alpha * h_ref[...] + ffn, g_ref[...], b_ref[...])


def _peer_ffn(h, u, v_t, s1, s2, e1, e2, tau, g, b, alpha):
    n, d = h.shape
    t = PEER_TOKENS
    rows = PEER_ROWS_PER_CHUNK
    ec_size = rows * PEER_NKEYS
    n_chunks = u.shape[0] // ec_size
    tok = pl.BlockSpec((t, d), lambda i, e: (i, 0))
    chunk_rows = pl.BlockSpec((PEER_HEADS, rows, t), lambda i, e: (0, e, i))
    all_rows = pl.BlockSpec((PEER_HEADS, PEER_NKEYS, t), lambda i, e: (0, 0, i))
    vec = pl.BlockSpec((1, d), lambda i, e: (0, 0))
    return pl.pallas_call(
        functools.partial(_peer_ffn_kernel, alpha=alpha),
        out_shape=jax.ShapeDtypeStruct((n, d), F32),
        grid=(n // t, n_chunks),
        in_specs=[tok,
                  pl.BlockSpec((ec_size, d), lambda i, e: (e, 0)),
                  pl.BlockSpec((d, ec_size), lambda i, e: (0, e)),
                  chunk_rows, chunk_rows, all_rows, all_rows,
                  pl.BlockSpec((PEER_HEADS, t), lambda i, e: (0, i)),
                  vec, vec],
        out_specs=tok,
        scratch_shapes=[pltpu.VMEM((d, t), F32), pltpu.VMEM((ec_size, t), F32), pltpu.VMEM((ec_size, t), BF16)],
        compiler_params=_params("parallel", "arbitrary"), name="peer_ffn",
    )(h, u, v_t, s1, e1, s2, e2, tau, g.reshape(1, d), b.reshape(1, d))


def _peer(h, w_q, sub_keys, u, v, g, b, alpha):
    q = _project(h, w_q.astype(BF16))
    keys = sub_keys.reshape(PEER_HEADS * 2, PEER_NKEYS, -1).astype(BF16)
    s1, s2, e1, e2, tau = _peer_select(q, keys)
    return _peer_ffn(h, u.astype(BF16), v.T.astype(BF16), s1, s2, e1, e2, tau, g, b, alpha)


def _t5_bucket(dist):
    max_exact = T5_BUCKETS // 2
    n = jnp.maximum(dist, 0)
    nf = jnp.maximum(n, 1).astype(F32)
    large = max_exact + (jnp.log(nf / max_exact) / math.log(T5_MAX_DIST / max_exact)
                         * (T5_BUCKETS - max_exact)).astype(jnp.int32)
    large = jnp.minimum(large, T5_BUCKETS - 1)
    return jnp.where(n < max_exact, n, large)


def _t5_near_tiles(t5_bias):
    key = jnp.arange(TK)[:, None]
    qry = jnp.arange(TQ)[None, :]
    tiles = []
    for off in range(2):
        dist = off * TK + qry - key
        bias = t5_bias[_t5_bucket(dist)] - t5_bias[T5_BUCKETS - 1]
        bias = jnp.where((dist >= 0)[..., None], bias, NEG)
        tiles.append(jnp.moveaxis(bias, -1, 0))
    return jnp.stack(tiles, axis=1).astype(F32)


def _dilated_log_multiplicity():
    key = np.arange(TK)[:, None]
    qry = np.arange(TQ)[None, :]
    tiles = []
    for off in range(N_DIL_OFFSETS):
        d = off * TK + qry - key
        count = np.zeros(d.shape, np.float64)
        for window, dil in DILATED_PATTERNS:
            count += (d >= 0) & (d <= window) & (d % dil == 0)
        tiles.append(np.where(count > 0, np.log(np.maximum(count, 1.0)), NEG))
    return np.stack(tiles).astype(np.float32)


def _dilated_tiles(t5_near):
    log_mult = jnp.asarray(_dilated_log_multiplicity())
    near = t5_near + log_mult[None, :2]
    far = jnp.broadcast_to(log_mult[None, 2:], (t5_near.shape[0],) + log_mult[2:].shape)
    return jnp.concatenate([near, far], axis=1)


def _attn_weights(w_in, gw, q_scales):
    grp = lambda g: w_in[:, g * gw:(g + 1) * gw]
    w_tok = jnp.concatenate([grp(1), grp(4)], axis=1).astype(BF16)
    w_feat_t = jnp.concatenate([grp(0) * q_scales[0], grp(2), grp(3) * q_scales[1], grp(5)], axis=1).T.astype(BF16)
    return w_tok, w_feat_t


def kernel(x, t5_bias, ab_w_in, ab_lambda, ab_subln_g, ab_w_out, cd_w_in, cd_b_f, cd_w_out,
           peer_w_q, peer_sub_keys, peer_u, peer_v, ln_g, ln_b):
    bsz, seq, d = x.shape
    depth = ln_g.shape[0]
    alpha = (2 * depth) ** 0.25
    n = bsz * seq
    gw = N_HEADS * HEAD_DIM
    blocks = gw // LANES
    h = x.reshape(n, d)
    t5_tiles = _t5_near_tiles(t5_bias)
    dil_tiles = _dilated_tiles(t5_tiles)
    for layer in range(depth):
        j = layer // 2
        if layer % 2 == 0:
            lambda_init = 0.8 - 0.6 * math.exp(-0.3 * layer)
            w_tok, w_feat_t = _attn_weights(ab_w_in[j], gw, (A_QK_DIM ** -0.5, HEAD_DIM ** -0.5))
            tok, feat = _project_attn(h, w_tok, w_feat_t)
            tok = tok.reshape(bsz, seq, -1)
            lv = ab_lambda[j].astype(F32)
            lam = jnp.exp(jnp.sum(lv[0] * lv[1])) - jnp.exp(jnp.sum(lv[2] * lv[3])) + lambda_init
            lam_row = jnp.full((1, TQ), lam, F32)
            g_tile = jnp.broadcast_to((ab_subln_g[j].astype(F32) * (1.0 - lambda_init))[:, None], (HEAD_DIM, TQ))
            o1 = _diff_attn(tok, feat, t5_tiles, lam_row, g_tile, 0, 0, blocks)
            o2 = _sb_attn(tok, feat, 2 * blocks, blocks, 3 * blocks)
            w_out = ab_w_out[j]
        else:
            w_tok, w_feat_t = _attn_weights(cd_w_in[j], gw, (HEAD_DIM ** -0.5, HEAD_DIM ** -0.5))
            n_gates = cd_w_in.shape[-1] - 6 * gw
            w_gate = jnp.zeros((d, LANES), F32).at[:, :n_gates].set(cd_w_in[j][:, 6 * gw:]).astype(BF16)
            tok, feat, gate = _project_attn(h, w_tok, w_feat_t, w_gate)
            tok = tok.reshape(bsz, seq, -1)
            f_rep = _fox_gates(gate, cd_b_f[j].astype(F32), bsz, seq)
            o1 = _dil_attn(tok, feat, dil_tiles, 0, 0, blocks)
            o2 = _fox_attn(tok, feat, f_rep, 2 * blocks, blocks, 3 * blocks)
            w_out = cd_w_out[j]
        h = _mix_ln(o1.reshape(n, gw), o2.reshape(n, gw), w_out.astype(BF16), h,
                    ln_g[layer, 0], ln_b[layer, 0], alpha)
        h = _peer(h, peer_w_q[layer], peer_sub_keys[layer], peer_u[layer], peer_v[layer],
                  ln_g[layer, 1], ln_b[layer, 1], alpha)
    return h.reshape(bsz, seq, d)
```

```python
import functools
import math

import numpy as np
import jax
import jax.numpy as jnp
from jax import lax
from jax.experimental import pallas as pl
from jax.experimental.pallas import tpu as pltpu

F32 = jnp.float32
BF16 = jnp.bfloat16

HEAD_DIM = 64
N_HEADS = 8
A_QK_DIM = 32
T5_BUCKETS = 32
T5_MAX_DIST = 128
DILATED_PATTERNS = ((128, 1), (512, 4), (2048, 16))
PEER_HEADS = 8
PEER_NKEYS = 128
PEER_TOPK = 16
LN_EPS = 1e-5

LANES = 128
HEADS_PER_TILE = LANES // HEAD_DIM
N_PAIRS = N_HEADS // HEADS_PER_TILE
TQ = 256
TK = 256
ROW_TILE = 512
GATE_TOKENS = 512
PEER_SEL_TOKENS = 256
PEER_TOKENS = 512
PEER_ROWS_PER_CHUNK = 8
VMEM_LIMIT_BYTES = 56 * 1024 * 1024
NEG = -1e30
SB_CUTOFF = -110.0


def _nt_dot(a, b):
    return lax.dot_general(a, b, (((1,), (1,)), ((), ())), preferred_element_type=F32)


def _dot(a, b):
    return jnp.dot(a, b, preferred_element_type=F32)


def _params(*sem):
    return pltpu.CompilerParams(dimension_semantics=sem, vmem_limit_bytes=VMEM_LIMIT_BYTES)


def _layer_norm(y, g, b):
    mu = jnp.mean(y, axis=-1, keepdims=True)
    yc = y - mu
    var = jnp.mean(yc * yc, axis=-1, keepdims=True)
    return yc * lax.rsqrt(var + LN_EPS) * g + b


def _log_sigmoid_pair(z):
    sp = jnp.log(1.0 + jnp.exp(-jnp.abs(z)))
    return jnp.minimum(z, 0.0) - sp, -jnp.maximum(z, 0.0) - sp


def _proj_kernel(x_ref, w_ref, o_ref, *, tn):
    x = x_ref[...].astype(BF16)
    for c in range(w_ref.shape[1] // tn):
        o_ref[:, c * tn:(c + 1) * tn] = _dot(x, w_ref[:, c * tn:(c + 1) * tn]).astype(o_ref.dtype)


def _project(x, w, tn=512):
    n, k = x.shape
    m = w.shape[1]
    tm = ROW_TILE
    return pl.pallas_call(
        functools.partial(_proj_kernel, tn=tn),
        out_shape=jax.ShapeDtypeStruct((n, m), BF16),
        grid=(n // tm,),
        in_specs=[pl.BlockSpec((tm, k), lambda i: (i, 0)), pl.BlockSpec((k, m), lambda i: (0, 0))],
        out_specs=pl.BlockSpec((tm, m), lambda i: (i, 0)),
        compiler_params=_params("parallel"), name="proj")(x, w)


def _proj_attn_kernel(x_ref, wt_ref, wf_ref, *rest, tn, has_gate):
    if has_gate:
        wg_ref, tok_ref, feat_ref, gate_ref = rest
    else:
        tok_ref, feat_ref = rest
    x = x_ref[...].astype(BF16)
    for c in range(wt_ref.shape[1] // tn):
        sl = slice(c * tn, (c + 1) * tn)
        tok_ref[:, sl] = _dot(x, wt_ref[:, sl]).astype(BF16)
    for c in range(wf_ref.shape[0] // tn):
        sl = slice(c * tn, (c + 1) * tn)
        feat_ref[sl, :] = _nt_dot(wf_ref[sl, :], x).astype(BF16)
    if has_gate:
        gate_ref[...] = _dot(x, wg_ref[...])


def _project_attn(x, w_tok, w_feat_t, w_gate=None, tn=512):
    n, k = x.shape
    mt, mf = w_tok.shape[1], w_feat_t.shape[0]
    tm = ROW_TILE
    has_gate = w_gate is not None
    in_specs = [pl.BlockSpec((tm, k), lambda i: (i, 0)),
                pl.BlockSpec((k, mt), lambda i: (0, 0)),
                pl.BlockSpec((mf, k), lambda i: (0, 0))]
    out_shape = [jax.ShapeDtypeStruct((n, mt), BF16), jax.ShapeDtypeStruct((mf, n), BF16)]
    out_specs = [pl.BlockSpec((tm, mt), lambda i: (i, 0)), pl.BlockSpec((mf, tm), lambda i: (0, i))]
    args = [x, w_tok, w_feat_t]
    if has_gate:
        in_specs.append(pl.BlockSpec((k, LANES), lambda i: (0, 0)))
        out_shape.append(jax.ShapeDtypeStruct((n, LANES), F32))
        out_specs.append(pl.BlockSpec((tm, LANES), lambda i: (i, 0)))
        args.append(w_gate)
    return pl.pallas_call(
        functools.partial(_proj_attn_kernel, tn=tn, has_gate=has_gate),
        out_shape=tuple(out_shape), grid=(n // tm,), in_specs=in_specs, out_specs=tuple(out_specs),
        compiler_params=_params("parallel"), name="proj_attn")(*args)


def _mix_ln_kernel(oa_ref, ob_ref, wa_ref, wb_ref, h_ref, g_ref, b_ref, o_ref, *, alpha):
    mix = _dot(oa_ref[...], wa_ref[...]) + _dot(ob_ref[...], wb_ref[...])
    o_ref[...] = _layer_norm(alpha * h_ref[...] + mix, g_ref[...], b_ref[...])


def _mix_ln(oa, ob, w_out, h, g, b, alpha):
    n, d = h.shape
    wa_rows = oa.shape[1]
    tm = ROW_TILE
    row = lambda width: pl.BlockSpec((tm, width), lambda i: (i, 0))
    full = lambda r, c: pl.BlockSpec((r, c), lambda i: (0, 0))
    return pl.pallas_call(
        functools.partial(_mix_ln_kernel, alpha=alpha),
        out_shape=jax.ShapeDtypeStruct((n, d), F32),
        grid=(n // tm,),
        in_specs=[row(wa_rows), row(ob.shape[1]), full(wa_rows, d), full(ob.shape[1], d),
                  row(d), full(1, d), full(1, d)],
        out_specs=row(d),
        compiler_params=_params("parallel"), name="mix_ln",
    )(oa, ob, w_out[:wa_rows], w_out[wa_rows:], h, g.reshape(1, d), b.reshape(1, d))


def _row_group_mask(shape, width, index):
    row = lax.broadcasted_iota(jnp.int32, shape, 0)
    return (row // width) == index


def _masked_queries(qt_ref, width):
    qt = qt_ref[...].astype(F32)
    return [jnp.where(_row_group_mask(qt.shape, width, j), qt, 0.0).astype(BF16)
            for j in range(LANES // width)]


def _score_fn(k_ref, qm):
    def qk(j, s_ref):
        k = k_ref[pl.ds(pl.multiple_of(j * TK, TK), TK), :]
        for mp, q in enumerate(qm):
            s_ref[mp] = _dot(k, q)
    return qk


def _alternate(lo, hi, bufs, step):
    def pair(t, carry):
        i = lo + 2 * t
        step(i, bufs[0], bufs[1])

        @pl.when(i + 1 < hi)
        def _():
            step(i + 1, bufs[1], bufs[0])
        return carry

    lax.fori_loop(0, (hi - lo + 1) // 2, pair, 0)


def _flash_block(i, first_block, n_blocks, qk, s_read, s_write, vt_ref, heads_of_map, adjust,
                 m_ref, l_ref, acc_ref):
    qk(first_block - jnp.minimum(i + 1, n_blocks - 1), s_write)
    start = pl.multiple_of((first_block - i) * TK, TK)
    vt = vt_ref[:, pl.ds(start, TK)]
    for mp, h in enumerate(heads_of_map):
        st = adjust(mp, s_read[mp], start)
        m_old = m_ref[mp]
        m_new = jnp.maximum(m_old, jnp.max(st, axis=0, keepdims=True))
        alpha = jnp.exp(m_old - m_new)
        p = jnp.exp(st - m_new)
        l_ref[mp] = alpha * l_ref[mp] + jnp.sum(p, axis=0, keepdims=True)
        pv = _dot(vt[h * HEAD_DIM:(h + 1) * HEAD_DIM, :], p.astype(BF16))
        acc_ref[mp] = alpha * acc_ref[mp] + pv
        m_ref[mp] = m_new


def _attn_specs(q_row, k_col, v_row, seq):
    nq = seq // TQ
    q_spec = pl.BlockSpec((LANES, TQ), lambda b, hp, qi: (q_row + hp, b * nq + qi))
    k_spec = pl.BlockSpec((None, seq, LANES), lambda b, hp, qi: (b, 0, k_col + hp))
    v_spec = pl.BlockSpec((LANES, seq), lambda b, hp, qi: (v_row + hp, b))
    o_spec = pl.BlockSpec((None, TQ, LANES), lambda b, hp, qi: (b, qi, hp))
    return q_spec, k_spec, v_spec, o_spec


def _attn_out_shape(tok):
    return jax.ShapeDtypeStruct((tok.shape[0], tok.shape[1], N_HEADS * HEAD_DIM), BF16)


def _softmax_scratch(n):
    return [pltpu.VMEM((n, TK, TQ), F32), pltpu.VMEM((n, TK, TQ), F32), pltpu.VMEM((n, 1, TQ), F32),
            pltpu.VMEM((n, 1, TQ), F32), pltpu.VMEM((n, HEAD_DIM, TQ), F32)]


def _init_softmax(m_ref, l_ref, acc_ref):
    m_ref[...] = jnp.full(m_ref.shape, NEG, F32)
    l_ref[...] = jnp.zeros(l_ref.shape, F32)
    acc_ref[...] = jnp.zeros(acc_ref.shape, F32)


def _store_heads(o_ref, head_rows):
    o_ref[...] = jnp.concatenate(head_rows, axis=0).T.astype(o_ref.dtype)


_ATTN_SEM = ("parallel", "parallel", "arbitrary")


def _diff_attn_kernel(qt_ref, k_ref, vt_ref, bias_ref, lam_ref, g_ref, o_ref, s_a, s_b, m_ref, l_ref, acc_ref):
    qi = pl.program_id(2)
    qk = _score_fn(k_ref, _masked_queries(qt_ref, A_QK_DIM))
    heads_of_map = (0, 0, 1, 1)
    _init_softmax(m_ref, l_ref, acc_ref)
    n_blocks = qi + 1
    qk(qi, s_a)

    def block(i, s_read, s_write, adjust):
        _flash_block(i, qi, n_blocks, qk, s_read, s_write, vt_ref, heads_of_map, adjust, m_ref, l_ref, acc_ref)

    block(0, s_a, s_b, lambda mp, st, start: st + bias_ref[mp // 2, 0])

    @pl.when(qi >= 1)
    def _():
        block(1, s_b, s_a, lambda mp, st, start: st + bias_ref[mp // 2, 1])

    _alternate(2, n_blocks, (s_a, s_b), lambda i, r, w: block(i, r, w, lambda mp, st, start: st))

    lam = lam_ref[...]
    heads = []
    for h in range(HEADS_PER_TILE):
        o = acc_ref[2 * h] / l_ref[2 * h] - lam * (acc_ref[2 * h + 1] / l_ref[2 * h + 1])
        ms = jnp.mean(o * o, axis=0, keepdims=True)
        heads.append(o * lax.rsqrt(ms + LN_EPS) * g_ref[...])
    _store_heads(o_ref, heads)


def _diff_attn(tok, feat, bias_tiles, lam_row, g_tile, q_row, k_col, v_row):
    b, s, _ = tok.shape
    q_spec, k_spec, v_spec, o_spec = _attn_specs(q_row, k_col, v_row, s)
    bias_spec = pl.BlockSpec((HEADS_PER_TILE, 2, TK, TQ), lambda b, hp, qi: (hp, 0, 0, 0))
    return pl.pallas_call(
        _diff_attn_kernel, out_shape=_attn_out_shape(tok),
        grid=(b, N_PAIRS, s // TQ),
        in_specs=[q_spec, k_spec, v_spec, bias_spec,
                  pl.BlockSpec((1, TQ), lambda b, hp, qi: (0, 0)),
                  pl.BlockSpec((HEAD_DIM, TQ), lambda b, hp, qi: (0, 0))],
        out_specs=o_spec, scratch_shapes=_softmax_scratch(LANES // A_QK_DIM),
        compiler_params=_params(*_ATTN_SEM), name="diff_attn",
    )(feat, tok, feat, bias_tiles, lam_row, g_tile)


def _sb_attn_kernel(qt_ref, k_ref, vt_ref, o_ref, s_a, s_b, c_ref, acc_ref):
    qi = pl.program_id(2)
    qk = _score_fn(k_ref, _masked_queries(qt_ref, HEAD_DIM))
    key = lax.broadcasted_iota(jnp.int32, (TK, TQ), 0)
    qry = lax.broadcasted_iota(jnp.int32, (TK, TQ), 1)
    past = key < qry
    later = (qry > key).astype(BF16)
    c_ref[...] = jnp.zeros(c_ref.shape, F32)
    acc_ref[...] = jnp.zeros(acc_ref.shape, F32)
    qk(qi, s_a)

    def block(i, s_read, s_write, diag):
        j = qi - i
        qk(jnp.maximum(j - 1, 0), s_write)
        vt = vt_ref[:, pl.ds(pl.multiple_of(j * TK, TK), TK)]
        c_max = []
        for h in range(HEADS_PER_TILE):
            log_beta, log_rem = _log_sigmoid_pair(s_read[h])
            if diag:
                log_rem = jnp.where(past, log_rem, 0.0)
            hi = log_rem.astype(BF16)
            lo = (log_rem - hi.astype(F32)).astype(BF16)
            after = _dot(later, hi) + _dot(later, lo) + c_ref[h]
            w = jnp.exp(log_beta + after)
            if diag:
                w = jnp.where(past, w, 0.0)
            acc_ref[h] += _dot(vt[h * HEAD_DIM:(h + 1) * HEAD_DIM, :], w.astype(BF16))
            c_new = c_ref[h] + jnp.sum(log_rem, axis=0, keepdims=True)
            c_ref[h] = c_new
            c_max.append(jnp.max(c_new))
        return jnp.maximum(c_max[0], c_max[1])

    first = block(0, s_a, s_b, True)

    def cond(state):
        return jnp.logical_and(state[0] <= qi, state[1] > SB_CUTOFF)

    def body(state):
        i = state[0]
        c_first = block(i, s_b, s_a, False)
        c_second = lax.cond(i + 1 <= qi, lambda: block(i + 1, s_a, s_b, False), lambda: c_first)
        return i + 2, c_second

    lax.while_loop(cond, body, (jnp.int32(1), first))
    _store_heads(o_ref, [acc_ref[h] for h in range(HEADS_PER_TILE)])


def _sb_attn(tok, feat, q_row, k_col, v_row):
    b, s, _ = tok.shape
    q_spec, k_spec, v_spec, o_spec = _attn_specs(q_row, k_col, v_row, s)
    return pl.pallas_call(
        _sb_attn_kernel, out_shape=_attn_out_shape(tok),
        grid=(b, N_PAIRS, s // TQ),
        in_specs=[q_spec, k_spec, v_spec], out_specs=o_spec,
        scratch_shapes=[pltpu.VMEM((HEADS_PER_TILE, TK, TQ), F32), pltpu.VMEM((HEADS_PER_TILE, TK, TQ), F32),
                        pltpu.VMEM((HEADS_PER_TILE, 1, TQ), F32), pltpu.VMEM((HEADS_PER_TILE, HEAD_DIM, TQ), F32)],
        compiler_params=_params(*_ATTN_SEM), name="sb_attn",
    )(feat, tok, feat)


N_DIL_OFFSETS = max(w for w, _ in DILATED_PATTERNS) // TK + 1


def _dil_attn_kernel(qt_ref, k_ref, vt_ref, tile_ref, o_ref, s_a, s_b, m_ref, l_ref, acc_ref):
    qi = pl.program_id(2)
    qk = _score_fn(k_ref, _masked_queries(qt_ref, HEAD_DIM))
    _init_softmax(m_ref, l_ref, acc_ref)
    n_blocks = jnp.minimum(qi, N_DIL_OFFSETS - 1) + 1
    qk(qi, s_a)

    def block(i, s_read, s_write):
        _flash_block(i, qi, n_blocks, qk, s_read, s_write, vt_ref, (0, 1),
                     lambda mp, st, start: st + tile_ref[mp, i], m_ref, l_ref, acc_ref)

    _alternate(0, n_blocks, (s_a, s_b), block)
    _store_heads(o_ref, [acc_ref[h] / l_ref[h] for h in range(HEADS_PER_TILE)])


def _dil_attn(tok, feat, tiles, q_row, k_col, v_row):
    b, s, _ = tok.shape
    q_spec, k_spec, v_spec, o_spec = _attn_specs(q_row, k_col, v_row, s)
    tile_spec = pl.BlockSpec((HEADS_PER_TILE, N_DIL_OFFSETS, TK, TQ), lambda b, hp, qi: (hp, 0, 0, 0))
    return pl.pallas_call(
        _dil_attn_kernel, out_shape=_attn_out_shape(tok),
        grid=(b, N_PAIRS, s // TQ),
        in_specs=[q_spec, k_spec, v_spec, tile_spec],
        out_specs=o_spec, scratch_shapes=_softmax_scratch(HEADS_PER_TILE),
        compiler_params=_params(*_ATTN_SEM), name="dil_attn",
    )(feat, tok, feat, tiles)


def _fox_gate_kernel(g_ref, bf_ref, f_ref, carry_ref):
    @pl.when(pl.program_id(1) == 0)
    def _():
        carry_ref[...] = jnp.zeros(carry_ref.shape, F32)

    row = lax.broadcasted_iota(jnp.int32, (LANES, LANES), 0)
    col = lax.broadcasted_iota(jnp.int32, (LANES, LANES), 1)
    upto = (col <= row).astype(F32)
    exact = functools.partial(jnp.dot, preferred_element_type=F32, precision=lax.Precision.HIGHEST)
    for blk in range(g_ref.shape[0] // LANES):
        sl = slice(blk * LANES, (blk + 1) * LANES)
        log_f, _ = _log_sigmoid_pair(g_ref[sl, :] + bf_ref[...])
        cs = exact(upto, log_f) + carry_ref[...]
        carry_ref[...] = cs[LANES - 1:LANES, :]
        for h in range(f_ref.shape[0]):
            f_ref[h, sl, :] = exact(cs, (row == h).astype(F32))


def _fox_gates(gate, b_f, bsz, seq):
    n = gate.shape[0]
    heads = b_f.shape[0]
    t = GATE_TOKENS
    nt = seq // t
    bf_row = jnp.zeros((1, LANES), F32).at[0, :heads].set(b_f)
    return pl.pallas_call(
        _fox_gate_kernel, out_shape=jax.ShapeDtypeStruct((heads, n, LANES), F32),
        grid=(bsz, nt),
        in_specs=[pl.BlockSpec((t, LANES), lambda b, i: (b * nt + i, 0)),
                  pl.BlockSpec((1, LANES), lambda b, i: (0, 0))],
        out_specs=pl.BlockSpec((heads, t, LANES), lambda b, i: (0, b * nt + i, 0)),
        scratch_shapes=[pltpu.VMEM((1, LANES), F32)],
        compiler_params=_params("parallel", "arbitrary"), name="fox_gates",
    )(gate, bf_row)


def _fox_attn_kernel(qt_ref, k_ref, vt_ref, f_ref, o_ref, s_a, s_b, m_ref, l_ref, acc_ref):
    qi = pl.program_id(2)
    qk = _score_fn(k_ref, _masked_queries(qt_ref, HEAD_DIM))
    _init_softmax(m_ref, l_ref, acc_ref)
    n_blocks = qi + 1
    key = lax.broadcasted_iota(jnp.int32, (TK, TQ), 0)
    qry = lax.broadcasted_iota(jnp.int32, (TK, TQ), 1)
    causal = key <= qry
    qk(qi, s_a)

    def forget(mp, st, start):
        f = f_ref[mp, pl.ds(start, TK), :]
        return st - jnp.concatenate([f] * (TQ // LANES), axis=1)

    def block(i, s_read, s_write, adjust):
        _flash_block(i, qi, n_blocks, qk, s_read, s_write, vt_ref, (0, 1), adjust, m_ref, l_ref, acc_ref)

    block(0, s_a, s_b, lambda mp, st, start: jnp.where(causal, forget(mp, st, start), NEG))
    _alternate(1, n_blocks, (s_b, s_a), lambda i, r, w: block(i, r, w, forget))
    _store_heads(o_ref, [acc_ref[h] / l_ref[h] for h in range(HEADS_PER_TILE)])


def _fox_attn(tok, feat, f_rep, q_row, k_col, v_row):
    b, s, _ = tok.shape
    q_spec, k_spec, v_spec, o_spec = _attn_specs(q_row, k_col, v_row, s)
    f_spec = pl.BlockSpec((HEADS_PER_TILE, s, LANES), lambda b, hp, qi: (hp, b, 0))
    return pl.pallas_call(
        _fox_attn_kernel, out_shape=_attn_out_shape(tok),
        grid=(b, N_PAIRS, s // TQ),
        in_specs=[q_spec, k_spec, v_spec, f_spec],
        out_specs=o_spec, scratch_shapes=_softmax_scratch(HEADS_PER_TILE),
        compiler_params=_params(*_ATTN_SEM), name="fox_attn",
    )(feat, tok, feat, f_rep)


def _top_values(work, out_ref, count):
    for r in range(count):
        m = jnp.max(work, axis=0, keepdims=True)
        out_ref[r:r + 1, :] = m
        work = jnp.where(work == m, -jnp.inf, work)


def _peer_select_kernel(q_ref, keys_ref, s1_ref, s2_ref, e1_ref, e2_ref, tau_ref, a_ref, b_ref, sel_ref):
    k = PEER_TOPK
    sub = 8
    row_sub = lax.broadcasted_iota(jnp.int32, (sub, q_ref.shape[0]), 0)
    for h in range(PEER_HEADS):
        s = [_nt_dot(keys_ref[2 * h + c], q_ref[:, (2 * h + c) * LANES:(2 * h + c + 1) * LANES])
             for c in range(2)]
        _top_values(s[0], a_ref, k)
        _top_values(s[1], b_ref, k)
        a = a_ref[...]
        b = b_ref[...]
        groups = [a[0:1] + b]
        for r1 in range(1, sub):
            g = a[r1:r1 + 1] + b[0:sub]
            groups.append(jnp.where(row_sub < k // (r1 + 1), g, -jnp.inf))
        groups.append(a[sub:k] + b[0:1])
        _top_values(jnp.concatenate(groups, axis=0), sel_ref, k)
        sel = sel_ref[...]
        z = jnp.sum(jnp.exp(sel - sel[0:1]), axis=0, keepdims=True)
        s1_ref[h] = s[0]
        s2_ref[h] = s[1]
        e1_ref[h] = jnp.exp(s[0] - a[0:1]) / z
        e2_ref[h] = jnp.exp(s[1] - b[0:1])
        tau_ref[h:h + 1, :] = sel[k - 1:k]


def _peer_select(q, keys):
    n = q.shape[0]
    t = PEER_SEL_TOKENS
    hk = jax.ShapeDtypeStruct((PEER_HEADS, PEER_NKEYS, n), F32)
    hk_spec = pl.BlockSpec((PEER_HEADS, PEER_NKEYS, t), lambda i: (0, 0, i))
    return pl.pallas_call(
        _peer_select_kernel,
        out_shape=(hk, hk, hk, hk, jax.ShapeDtypeStruct((PEER_HEADS, n), F32)),
        grid=(n // t,),
        in_specs=[pl.BlockSpec((t, q.shape[1]), lambda i: (i, 0)),
                  pl.BlockSpec(keys.shape, lambda i: (0, 0, 0))],
        out_specs=(hk_spec, hk_spec, hk_spec, hk_spec, pl.BlockSpec((PEER_HEADS, t), lambda i: (0, i))),
        scratch_shapes=[pltpu.VMEM((PEER_TOPK, t), F32)] * 3,
        compiler_params=_params("parallel"), name="peer_select",
    )(q, keys)


def _gelu(a):
    return 0.5 * a * (1.0 + lax.erf(a * (1.0 / math.sqrt(2.0))))


def _peer_ffn_kernel(h_ref, u_ref, vt_ref, s1_ref, e1_ref, s2_ref, e2_ref, tau_ref, g_ref, b_ref,
                     o_ref, acc_ref, act_ref, w_ref, *, alpha):
    ec = pl.program_id(1)

    @pl.when(ec == 0)
    def _():
        acc_ref[...] = jnp.zeros(acc_ref.shape, F32)

    x = h_ref[...].astype(BF16)
    act_ref[...] = _nt_dot(u_ref[...], x)

    def rows(ii, carry):
        sl = pl.ds(pl.multiple_of(ii * PEER_NKEYS, PEER_NKEYS), PEER_NKEYS)
        gate = jnp.zeros((PEER_NKEYS, x.shape[0]), F32)
        for h in range(PEER_HEADS):
            pair_sum = s2_ref[h] + s1_ref[h, pl.ds(ii, 1), :]
            pair_gate = e2_ref[h] * e1_ref[h, pl.ds(ii, 1), :]
            gate = gate + jnp.where(pair_sum >= tau_ref[h:h + 1, :], pair_gate, 0.0)
        w_ref[sl, :] = (gate * _gelu(act_ref[sl, :])).astype(BF16)
        return carry

    lax.fori_loop(0, PEER_ROWS_PER_CHUNK, rows, 0)
    acc_ref[...] += _dot(vt_ref[...], w_ref[...])

    @pl.when(ec == pl.num_programs(1) - 1)
    def _():
        ffn = acc_ref[...].T
        o_ref[...] = _layer_norm(alpha * h_ref[...] + ffn, g_ref[...], b_ref[...])


def _peer_ffn(h, u, v_t, s1, s2, e1, e2, tau, g, b, alpha):
    n, d = h.shape
    t = PEER_TOKENS
    rows = PEER_ROWS_PER_CHUNK
    ec_size = rows * PEER_NKEYS
    n_chunks = u.shape[0] // ec_size
    tok = pl.BlockSpec((t, d), lambda i, e: (i, 0))
    chunk_rows = pl.BlockSpec((PEER_HEADS, rows, t), lambda i, e: (0, e, i))
    all_rows = pl.BlockSpec((PEER_HEADS, PEER_NKEYS, t), lambda i, e: (0, 0, i))
    vec = pl.BlockSpec((1, d), lambda i, e: (0, 0))
    return pl.pallas_call(
        functools.partial(_peer_ffn_kernel, alpha=alpha),
        out_shape=jax.ShapeDtypeStruct((n, d), F32),
        grid=(n // t, n_chunks),
        in_specs=[tok,
                  pl.BlockSpec((ec_size, d), lambda i, e: (e, 0)),
                  pl.BlockSpec((d, ec_size), lambda i, e: (0, e)),
                  chunk_rows, chunk_rows, all_rows, all_rows,
                  pl.BlockSpec((PEER_HEADS, t), lambda i, e: (0, i)),
                  vec, vec],
        out_specs=tok,
        scratch_shapes=[pltpu.VMEM((d, t), F32), pltpu.VMEM((ec_size, t), F32), pltpu.VMEM((ec_size, t), BF16)],
        compiler_params=_params("parallel", "arbitrary"), name="peer_ffn",
    )(h, u, v_t, s1, e1, s2, e2, tau, g.reshape(1, d), b.reshape(1, d))


def _transpose_cast_kernel(x_ref, o_ref):
    o_ref[...] = x_ref[...].T.astype(o_ref.dtype)


def _transpose_cast(x):
    rows, cols = x.shape
    t = ROW_TILE
    return pl.pallas_call(
        _transpose_cast_kernel, out_shape=jax.ShapeDtypeStruct((cols, rows), BF16),
        grid=(rows // t,),
        in_specs=[pl.BlockSpec((t, cols), lambda i: (i, 0))],
        out_specs=pl.BlockSpec((cols, t), lambda i: (0, i)),
        compiler_params=_params("parallel"), name="transpose_cast")(x)


def _peer(h, w_q, sub_keys, u, v, g, b, alpha):
    q = _project(h, w_q.astype(BF16))
    keys = sub_keys.reshape(PEER_HEADS * 2, PEER_NKEYS, -1).astype(BF16)
    s1, s2, e1, e2, tau = _peer_select(q, keys)
    return _peer_ffn(h, u.astype(BF16), _transpose_cast(v), s1, s2, e1, e2, tau, g, b, alpha)


def _t5_bucket(dist):
    max_exact = T5_BUCKETS // 2
    n = jnp.maximum(dist, 0)
    nf = jnp.maximum(n, 1).astype(F32)
    large = max_exact + (jnp.log(nf / max_exact) / math.log(T5_MAX_DIST / max_exact)
                         * (T5_BUCKETS - max_exact)).astype(jnp.int32)
    large = jnp.minimum(large, T5_BUCKETS - 1)
    return jnp.where(n < max_exact, n, large)


def _t5_near_tiles(t5_bias):
    key = jnp.arange(TK)[:, None]
    qry = jnp.arange(TQ)[None, :]
    tiles = []
    for off in range(2):
        dist = off * TK + qry - key
        bias = t5_bias[_t5_bucket(dist)] - t5_bias[T5_BUCKETS - 1]
        bias = jnp.where((dist >= 0)[..., None], bias, NEG)
        tiles.append(jnp.moveaxis(bias, -1, 0))
    return jnp.stack(tiles, axis=1).astype(F32)


def _dilated_log_multiplicity():
    key = np.arange(TK)[:, None]
    qry = np.arange(TQ)[None, :]
    tiles = []
    for off in range(N_DIL_OFFSETS):
        d = off * TK + qry - key
        count = np.zeros(d.shape, np.float64)
        for window, dil in DILATED_PATTERNS:
            count += (d >= 0) & (d <= window) & (d % dil == 0)
        tiles.append(np.where(count > 0, np.log(np.maximum(count, 1.0)), NEG))
    return np.stack(tiles).astype(np.float32)


def _dilated_tiles(t5_near):
    log_mult = jnp.asarray(_dilated_log_multiplicity())
    near = t5_near + log_mult[None, :2]
    far = jnp.broadcast_to(log_mult[None, 2:], (t5_near.shape[0],) + log_mult[2:].shape)
    return jnp.concatenate([near, far], axis=1)


def _attn_weights(w_in, gw, q_scales):
    grp = lambda g: w_in[:, g * gw:(g + 1) * gw]
    w_tok = jnp.concatenate([grp(1), grp(4)], axis=1).astype(BF16)
    w_feat_t = jnp.concatenate([grp(0) * q_scales[0], grp(2), grp(3) * q_scales[1], grp(5)], axis=1).T.astype(BF16)
    return w_tok, w_feat_t


def kernel(x, t5_bias, ab_w_in, ab_lambda, ab_subln_g, ab_w_out, cd_w_in, cd_b_f, cd_w_out,
           peer_w_q, peer_sub_keys, peer_u, peer_v, ln_g, ln_b):
    bsz, seq, d = x.shape
    depth = ln_g.shape[0]
    alpha = (2 * depth) ** 0.25
    n = bsz * seq
    gw = N_HEADS * HEAD_DIM
    blocks = gw // LANES
    h = x.reshape(n, d)
    t5_tiles = _t5_near_tiles(t5_bias)
    dil_tiles = _dilated_tiles(t5_tiles)
    for layer in range(depth):
        j = layer // 2
        if layer % 2 == 0:
            lambda_init = 0.8 - 0.6 * math.exp(-0.3 * layer)
            w_tok, w_feat_t = _attn_weights(ab_w_in[j], gw, (A_QK_DIM ** -0.5, HEAD_DIM ** -0.5))
            tok, feat = _project_attn(h, w_tok, w_feat_t)
            tok = tok.reshape(bsz, seq, -1)
            lv = ab_lambda[j].astype(F32)
            lam = jnp.exp(jnp.sum(lv[0] * lv[1])) - jnp.exp(jnp.sum(lv[2] * lv[3])) + lambda_init
            lam_row = jnp.full((1, TQ), lam, F32)
            g_tile = jnp.broadcast_to((ab_subln_g[j].astype(F32) * (1.0 - lambda_init))[:, None], (HEAD_DIM, TQ))
            o1 = _diff_attn(tok, feat, t5_tiles, lam_row, g_tile, 0, 0, blocks)
            o2 = _sb_attn(tok, feat, 2 * blocks, blocks, 3 * blocks)
            w_out = ab_w_out[j]
        else:
            w_tok, w_feat_t = _attn_weights(cd_w_in[j], gw, (HEAD_DIM ** -0.5, HEAD_DIM ** -0.5))
            n_gates = cd_w_in.shape[-1] - 6 * gw
            w_gate = jnp.zeros((d, LANES), F32).at[:, :n_gates].set(cd_w_in[j][:, 6 * gw:]).astype(BF16)
            tok, feat, gate = _project_attn(h, w_tok, w_feat_t, w_gate)
            tok = tok.reshape(bsz, seq, -1)
            f_rep = _fox_gates(gate, cd_b_f[j].astype(F32), bsz, seq)
            o1 = _dil_attn(tok, feat, dil_tiles, 0, 0, blocks)
            o2 = _fox_attn(tok, feat, f_rep, 2 * blocks, blocks, 3 * blocks)
            w_out = cd_w_out[j]
        h = _mix_ln(o1.reshape(n, gw), o2.reshape(n, gw), w_out.astype(BF16), h,
                    ln_g[layer, 0], ln_b[layer, 0], alpha)
        h = _peer(h, peer_w_q[layer], peer_sub_keys[layer], peer_u[layer], peer_v[layer],
                  ln_g[layer, 1], ln_b[layer, 1], alpha)
    return h.reshape(bsz, seq, d)
```

```python
import functools
import math

import numpy as np
import jax
import jax.numpy as jnp
from jax import lax
from jax.experimental import pallas as pl
from jax.experimental.pallas import tpu as pltpu

F32 = jnp.float32
BF16 = jnp.bfloat16

HEAD_DIM = 64
N_HEADS = 8
A_QK_DIM = 32
T5_BUCKETS = 32
T5_MAX_DIST = 128
DILATED_PATTERNS = ((128, 1), (512, 4), (2048, 16))
PEER_HEADS = 8
PEER_NKEYS = 128
PEER_TOPK = 16
LN_EPS = 1e-5

LANES = 128
HEADS_PER_TILE = LANES // HEAD_DIM
N_PAIRS = N_HEADS // HEADS_PER_TILE
TQ = 256
TK = 256
ROW_TILE = 512
GATE_TOKENS = 512
PEER_SEL_TOKENS = 256
PEER_TOKENS = 512
PEER_ROWS_PER_CHUNK = 8
PEER_GATE_SLAB = 16
VMEM_LIMIT_BYTES = 56 * 1024 * 1024
NEG = -1e30
SB_CUTOFF = -110.0


def _nt_dot(a, b):
    return lax.dot_general(a, b, (((1,), (1,)), ((), ())), preferred_element_type=F32)


def _dot(a, b):
    return jnp.dot(a, b, preferred_element_type=F32)


def _params(*sem):
    return pltpu.CompilerParams(dimension_semantics=sem, vmem_limit_bytes=VMEM_LIMIT_BYTES)


def _layer_norm(y, g, b):
    mu = jnp.mean(y, axis=-1, keepdims=True)
    yc = y - mu
    var = jnp.mean(yc * yc, axis=-1, keepdims=True)
    return yc * lax.rsqrt(var + LN_EPS) * g + b


def _log_sigmoid_pair(z):
    sp = jnp.log(1.0 + jnp.exp(-jnp.abs(z)))
    return jnp.minimum(z, 0.0) - sp, -jnp.maximum(z, 0.0) - sp


def _proj_kernel(x_ref, w_ref, o_ref, *, tn):
    x = x_ref[...].astype(BF16)
    for c in range(w_ref.shape[1] // tn):
        o_ref[:, c * tn:(c + 1) * tn] = _dot(x, w_ref[:, c * tn:(c + 1) * tn]).astype(o_ref.dtype)


def _project(x, w, tn=512):
    n, k = x.shape
    m = w.shape[1]
    tm = ROW_TILE
    return pl.pallas_call(
        functools.partial(_proj_kernel, tn=tn),
        out_shape=jax.ShapeDtypeStruct((n, m), BF16),
        grid=(n // tm,),
        in_specs=[pl.BlockSpec((tm, k), lambda i: (i, 0)), pl.BlockSpec((k, m), lambda i: (0, 0))],
        out_specs=pl.BlockSpec((tm, m), lambda i: (i, 0)),
        compiler_params=_params("parallel"), name="proj")(x, w)


def _proj_attn_kernel(x_ref, wt_ref, wf_ref, *rest, tn, has_gate):
    if has_gate:
        wg_ref, tok_ref, feat_ref, gate_ref = rest
    else:
        tok_ref, feat_ref = rest
    x = x_ref[...].astype(BF16)
    for c in range(wt_ref.shape[1] // tn):
        sl = slice(c * tn, (c + 1) * tn)
        tok_ref[:, sl] = _dot(x, wt_ref[:, sl]).astype(BF16)
    for c in range(wf_ref.shape[0] // tn):
        sl = slice(c * tn, (c + 1) * tn)
        feat_ref[sl, :] = _nt_dot(wf_ref[sl, :], x).astype(BF16)
    if has_gate:
        gate_ref[...] = _dot(x, wg_ref[...])


def _project_attn(x, w_tok, w_feat_t, w_gate=None, tn=512):
    n, k = x.shape
    mt, mf = w_tok.shape[1], w_feat_t.shape[0]
    tm = ROW_TILE
    has_gate = w_gate is not None
    in_specs = [pl.BlockSpec((tm, k), lambda i: (i, 0)),
                pl.BlockSpec((k, mt), lambda i: (0, 0)),
                pl.BlockSpec((mf, k), lambda i: (0, 0))]
    out_shape = [jax.ShapeDtypeStruct((n, mt), BF16), jax.ShapeDtypeStruct((mf, n), BF16)]
    out_specs = [pl.BlockSpec((tm, mt), lambda i: (i, 0)), pl.BlockSpec((mf, tm), lambda i: (0, i))]
    args = [x, w_tok, w_feat_t]
    if has_gate:
        in_specs.append(pl.BlockSpec((k, LANES), lambda i: (0, 0)))
        out_shape.append(jax.ShapeDtypeStruct((n, LANES), F32))
        out_specs.append(pl.BlockSpec((tm, LANES), lambda i: (i, 0)))
        args.append(w_gate)
    return pl.pallas_call(
        functools.partial(_proj_attn_kernel, tn=tn, has_gate=has_gate),
        out_shape=tuple(out_shape), grid=(n // tm,), in_specs=in_specs, out_specs=tuple(out_specs),
        compiler_params=_params("parallel"), name="proj_attn")(*args)


def _mix_ln_kernel(oa_ref, ob_ref, wa_ref, wb_ref, h_ref, g_ref, b_ref, o_ref, *, alpha):
    mix = _dot(oa_ref[...], wa_ref[...]) + _dot(ob_ref[...], wb_ref[...])
    o_ref[...] = _layer_norm(alpha * h_ref[...] + mix, g_ref[...], b_ref[...])


def _mix_ln(oa, ob, w_out, h, g, b, alpha):
    n, d = h.shape
    wa_rows = oa.shape[1]
    tm = ROW_TILE
    row = lambda width: pl.BlockSpec((tm, width), lambda i: (i, 0))
    full = lambda r, c: pl.BlockSpec((r, c), lambda i: (0, 0))
    return pl.pallas_call(
        functools.partial(_mix_ln_kernel, alpha=alpha),
        out_shape=jax.ShapeDtypeStruct((n, d), F32),
        grid=(n // tm,),
        in_specs=[row(wa_rows), row(ob.shape[1]), full(wa_rows, d), full(ob.shape[1], d),
                  row(d), full(1, d), full(1, d)],
        out_specs=row(d),
        compiler_params=_params("parallel"), name="mix_ln",
    )(oa, ob, w_out[:wa_rows], w_out[wa_rows:], h, g.reshape(1, d), b.reshape(1, d))


def _row_group_mask(shape, width, index):
    row = lax.broadcasted_iota(jnp.int32, shape, 0)
    return (row // width) == index


def _masked_queries(qt_ref, width):
    qt = qt_ref[...].astype(F32)
    return [jnp.where(_row_group_mask(qt.shape, width, j), qt, 0.0).astype(BF16)
            for j in range(LANES // width)]


def _score_fn(k_ref, qm):
    def qk(j, s_ref):
        k = k_ref[pl.ds(pl.multiple_of(j * TK, TK), TK), :]
        for mp, q in enumerate(qm):
            s_ref[mp] = _dot(k, q)
    return qk


def _alternate(lo, hi, bufs, step):
    def pair(t, carry):
        i = lo + 2 * t
        step(i, bufs[0], bufs[1])

        @pl.when(i + 1 < hi)
        def _():
            step(i + 1, bufs[1], bufs[0])
        return carry

    lax.fori_loop(0, (hi - lo + 1) // 2, pair, 0)


def _flash_block(i, first_block, n_blocks, qk, s_read, s_write, vt_ref, heads_of_map, adjust,
                 m_ref, l_ref, acc_ref):
    qk(first_block - jnp.minimum(i + 1, n_blocks - 1), s_write)
    start = pl.multiple_of((first_block - i) * TK, TK)
    vt = vt_ref[:, pl.ds(start, TK)]
    for mp, h in enumerate(heads_of_map):
        st = adjust(mp, s_read[mp], start)
        m_old = m_ref[mp]
        m_new = jnp.maximum(m_old, jnp.max(st, axis=0, keepdims=True))
        alpha = jnp.exp(m_old - m_new)
        p = jnp.exp(st - m_new)
        l_ref[mp] = alpha * l_ref[mp] + jnp.sum(p, axis=0, keepdims=True)
        pv = _dot(vt[h * HEAD_DIM:(h + 1) * HEAD_DIM, :], p.astype(BF16))
        acc_ref[mp] = alpha * acc_ref[mp] + pv
        m_ref[mp] = m_new


def _attn_specs(q_row, k_col, v_row, seq):
    nq = seq // TQ
    q_spec = pl.BlockSpec((LANES, TQ), lambda b, hp, qi: (q_row + hp, b * nq + qi))
    k_spec = pl.BlockSpec((None, seq, LANES), lambda b, hp, qi: (b, 0, k_col + hp))
    v_spec = pl.BlockSpec((LANES, seq), lambda b, hp, qi: (v_row + hp, b))
    o_spec = pl.BlockSpec((None, TQ, LANES), lambda b, hp, qi: (b, qi, hp))
    return q_spec, k_spec, v_spec, o_spec


def _attn_out_shape(tok):
    return jax.ShapeDtypeStruct((tok.shape[0], tok.shape[1], N_HEADS * HEAD_DIM), BF16)


def _softmax_scratch(n):
    return [pltpu.VMEM((n, TK, TQ), F32), pltpu.VMEM((n, TK, TQ), F32), pltpu.VMEM((n, 1, TQ), F32),
            pltpu.VMEM((n, 1, TQ), F32), pltpu.VMEM((n, HEAD_DIM, TQ), F32)]


def _init_softmax(m_ref, l_ref, acc_ref):
    m_ref[...] = jnp.full(m_ref.shape, NEG, F32)
    l_ref[...] = jnp.zeros(l_ref.shape, F32)
    acc_ref[...] = jnp.zeros(acc_ref.shape, F32)


def _store_heads(o_ref, head_rows):
    o_ref[...] = jnp.concatenate(head_rows, axis=0).T.astype(o_ref.dtype)


_ATTN_SEM = ("parallel", "parallel", "arbitrary")


def _diff_attn_kernel(qt_ref, k_ref, vt_ref, bias_ref, lam_ref, g_ref, o_ref, s_a, s_b, m_ref, l_ref, acc_ref):
    qi = pl.program_id(2)
    qk = _score_fn(k_ref, _masked_queries(qt_ref, A_QK_DIM))
    heads_of_map = (0, 0, 1, 1)
    _init_softmax(m_ref, l_ref, acc_ref)
    n_blocks = qi + 1
    qk(qi, s_a)

    def block(i, s_read, s_write, adjust):
        _flash_block(i, qi, n_blocks, qk, s_read, s_write, vt_ref, heads_of_map, adjust, m_ref, l_ref, acc_ref)

    block(0, s_a, s_b, lambda mp, st, start: st + bias_ref[mp // 2, 0])

    @pl.when(qi >= 1)
    def _():
        block(1, s_b, s_a, lambda mp, st, start: st + bias_ref[mp // 2, 1])

    _alternate(2, n_blocks, (s_a, s_b), lambda i, r, w: block(i, r, w, lambda mp, st, start: st))

    lam = lam_ref[...]
    heads = []
    for h in range(HEADS_PER_TILE):
        o = acc_ref[2 * h] / l_ref[2 * h] - lam * (acc_ref[2 * h + 1] / l_ref[2 * h + 1])
        ms = jnp.mean(o * o, axis=0, keepdims=True)
        heads.append(o * lax.rsqrt(ms + LN_EPS) * g_ref[...])
    _store_heads(o_ref, heads)


def _diff_attn(tok, feat, bias_tiles, lam_row, g_tile, q_row, k_col, v_row):
    b, s, _ = tok.shape
    q_spec, k_spec, v_spec, o_spec = _attn_specs(q_row, k_col, v_row, s)
    bias_spec = pl.BlockSpec((HEADS_PER_TILE, 2, TK, TQ), lambda b, hp, qi: (hp, 0, 0, 0))
    return pl.pallas_call(
        _diff_attn_kernel, out_shape=_attn_out_shape(tok),
        grid=(b, N_PAIRS, s // TQ),
        in_specs=[q_spec, k_spec, v_spec, bias_spec,
                  pl.BlockSpec((1, TQ), lambda b, hp, qi: (0, 0)),
                  pl.BlockSpec((HEAD_DIM, TQ), lambda b, hp, qi: (0, 0))],
        out_specs=o_spec, scratch_shapes=_softmax_scratch(LANES // A_QK_DIM),
        compiler_params=_params(*_ATTN_SEM), name="diff_attn",
    )(feat, tok, feat, bias_tiles, lam_row, g_tile)


def _sb_attn_kernel(qt_ref, k_ref, vt_ref, o_ref, s_a, s_b, c_ref, acc_ref):
    qi = pl.program_id(2)
    qk = _score_fn(k_ref, _masked_queries(qt_ref, HEAD_DIM))
    key = lax.broadcasted_iota(jnp.int32, (TK, TQ), 0)
    qry = lax.broadcasted_iota(jnp.int32, (TK, TQ), 1)
    past = key < qry
    later = (qry > key).astype(BF16)
    c_ref[...] = jnp.zeros(c_ref.shape, F32)
    acc_ref[...] = jnp.zeros(acc_ref.shape, F32)
    qk(qi, s_a)

    def block(i, s_read, s_write, diag):
        j = qi - i
        qk(jnp.maximum(j - 1, 0), s_write)
        vt = vt_ref[:, pl.ds(pl.multiple_of(j * TK, TK), TK)]
        c_max = []
        for h in range(HEADS_PER_TILE):
            log_beta, log_rem = _log_sigmoid_pair(s_read[h])
            if diag:
                log_rem = jnp.where(past, log_rem, 0.0)
            hi = log_rem.astype(BF16)
            lo = (log_rem - hi.astype(F32)).astype(BF16)
            after = _dot(later, hi) + _dot(later, lo) + c_ref[h]
            w = jnp.exp(log_beta + after)
            if diag:
                w = jnp.where(past, w, 0.0)
            acc_ref[h] += _dot(vt[h * HEAD_DIM:(h + 1) * HEAD_DIM, :], w.astype(BF16))
            c_new = c_ref[h] + jnp.sum(log_rem, axis=0, keepdims=True)
            c_ref[h] = c_new
            c_max.append(jnp.max(c_new))
        return jnp.maximum(c_max[0], c_max[1])

    first = block(0, s_a, s_b, True)

    def cond(state):
        return jnp.logical_and(state[0] <= qi, state[1] > SB_CUTOFF)

    def body(state):
        i = state[0]
        c_first = block(i, s_b, s_a, False)
        more = jnp.logical_and(i + 1 <= qi, c_first > SB_CUTOFF)
        c_second = lax.cond(more, lambda: block(i + 1, s_a, s_b, False), lambda: c_first)
        return i + 2, c_second

    lax.while_loop(cond, body, (jnp.int32(1), first))
    _store_heads(o_ref, [acc_ref[h] for h in range(HEADS_PER_TILE)])


def _sb_attn(tok, feat, q_row, k_col, v_row):
    b, s, _ = tok.shape
    q_spec, k_spec, v_spec, o_spec = _attn_specs(q_row, k_col, v_row, s)
    return pl.pallas_call(
        _sb_attn_kernel, out_shape=_attn_out_shape(tok),
        grid=(b, N_PAIRS, s // TQ),
        in_specs=[q_spec, k_spec, v_spec], out_specs=o_spec,
        scratch_shapes=[pltpu.VMEM((HEADS_PER_TILE, TK, TQ), F32), pltpu.VMEM((HEADS_PER_TILE, TK, TQ), F32),
                        pltpu.VMEM((HEADS_PER_TILE, 1, TQ), F32), pltpu.VMEM((HEADS_PER_TILE, HEAD_DIM, TQ), F32)],
        compiler_params=_params(*_ATTN_SEM), name="sb_attn",
    )(feat, tok, feat)


N_DIL_OFFSETS = max(w for w, _ in DILATED_PATTERNS) // TK + 1


def _dil_attn_kernel(qt_ref, k_ref, vt_ref, tile_ref, o_ref, s_a, s_b, m_ref, l_ref, acc_ref):
    qi = pl.program_id(2)
    qk = _score_fn(k_ref, _masked_queries(qt_ref, HEAD_DIM))
    _init_softmax(m_ref, l_ref, acc_ref)
    n_blocks = jnp.minimum(qi, N_DIL_OFFSETS - 1) + 1
    qk(qi, s_a)

    def block(i, s_read, s_write):
        _flash_block(i, qi, n_blocks, qk, s_read, s_write, vt_ref, (0, 1),
                     lambda mp, st, start: st + tile_ref[mp, i], m_ref, l_ref, acc_ref)

    _alternate(0, n_blocks, (s_a, s_b), block)
    _store_heads(o_ref, [acc_ref[h] / l_ref[h] for h in range(HEADS_PER_TILE)])


def _dil_attn(tok, feat, tiles, q_row, k_col, v_row):
    b, s, _ = tok.shape
    q_spec, k_spec, v_spec, o_spec = _attn_specs(q_row, k_col, v_row, s)
    tile_spec = pl.BlockSpec((HEADS_PER_TILE, N_DIL_OFFSETS, TK, TQ), lambda b, hp, qi: (hp, 0, 0, 0))
    return pl.pallas_call(
        _dil_attn_kernel, out_shape=_attn_out_shape(tok),
        grid=(b, N_PAIRS, s // TQ),
        in_specs=[q_spec, k_spec, v_spec, tile_spec],
        out_specs=o_spec, scratch_shapes=_softmax_scratch(HEADS_PER_TILE),
        compiler_params=_params(*_ATTN_SEM), name="dil_attn",
    )(feat, tok, feat, tiles)


def _fox_gate_kernel(g_ref, bf_ref, f_ref, carry_ref):
    @pl.when(pl.program_id(1) == 0)
    def _():
        carry_ref[...] = jnp.zeros(carry_ref.shape, F32)

    row = lax.broadcasted_iota(jnp.int32, (LANES, LANES), 0)
    col = lax.broadcasted_iota(jnp.int32, (LANES, LANES), 1)
    upto = (col <= row).astype(F32)
    exact = functools.partial(jnp.dot, preferred_element_type=F32, precision=lax.Precision.HIGHEST)
    for blk in range(g_ref.shape[0] // LANES):
        sl = slice(blk * LANES, (blk + 1) * LANES)
        log_f, _ = _log_sigmoid_pair(g_ref[sl, :] + bf_ref[...])
        cs = exact(upto, log_f) + carry_ref[...]
        carry_ref[...] = cs[LANES - 1:LANES, :]
        for h in range(f_ref.shape[0]):
            f_ref[h, sl, :] = exact(cs, (row == h).astype(F32))


def _fox_gates(gate, b_f, bsz, seq):
    n = gate.shape[0]
    heads = b_f.shape[0]
    t = GATE_TOKENS
    nt = seq // t
    bf_row = jnp.zeros((1, LANES), F32).at[0, :heads].set(b_f)
    return pl.pallas_call(
        _fox_gate_kernel, out_shape=jax.ShapeDtypeStruct((heads, n, LANES), F32),
        grid=(bsz, nt),
        in_specs=[pl.BlockSpec((t, LANES), lambda b, i: (b * nt + i, 0)),
                  pl.BlockSpec((1, LANES), lambda b, i: (0, 0))],
        out_specs=pl.BlockSpec((heads, t, LANES), lambda b, i: (0, b * nt + i, 0)),
        scratch_shapes=[pltpu.VMEM((1, LANES), F32)],
        compiler_params=_params("parallel", "arbitrary"), name="fox_gates",
    )(gate, bf_row)


def _fox_attn_kernel(qt_ref, k_ref, vt_ref, f_ref, o_ref, s_a, s_b, m_ref, l_ref, acc_ref):
    qi = pl.program_id(2)
    qk = _score_fn(k_ref, _masked_queries(qt_ref, HEAD_DIM))
    _init_softmax(m_ref, l_ref, acc_ref)
    n_blocks = qi + 1
    key = lax.broadcasted_iota(jnp.int32, (TK, TQ), 0)
    qry = lax.broadcasted_iota(jnp.int32, (TK, TQ), 1)
    causal = key <= qry
    qk(qi, s_a)

    def forget(mp, st, start):
        f = f_ref[mp, pl.ds(start, TK), :]
        return st - jnp.concatenate([f] * (TQ // LANES), axis=1)

    def block(i, s_read, s_write, adjust):
        _flash_block(i, qi, n_blocks, qk, s_read, s_write, vt_ref, (0, 1), adjust, m_ref, l_ref, acc_ref)

    block(0, s_a, s_b, lambda mp, st, start: jnp.where(causal, forget(mp, st, start), NEG))
    _alternate(1, n_blocks, (s_b, s_a), lambda i, r, w: block(i, r, w, forget))
    _store_heads(o_ref, [acc_ref[h] / l_ref[h] for h in range(HEADS_PER_TILE)])


def _fox_attn(tok, feat, f_rep, q_row, k_col, v_row):
    b, s, _ = tok.shape
    q_spec, k_spec, v_spec, o_spec = _attn_specs(q_row, k_col, v_row, s)
    f_spec = pl.BlockSpec((HEADS_PER_TILE, s, LANES), lambda b, hp, qi: (hp, b, 0))
    return pl.pallas_call(
        _fox_attn_kernel, out_shape=_attn_out_shape(tok),
        grid=(b, N_PAIRS, s // TQ),
        in_specs=[q_spec, k_spec, v_spec, f_spec],
        out_specs=o_spec, scratch_shapes=_softmax_scratch(HEADS_PER_TILE),
        compiler_params=_params(*_ATTN_SEM), name="fox_attn",
    )(feat, tok, feat, f_rep)


def _top_values(work, out_ref, count):
    for r in range(count):
        m = jnp.max(work, axis=0, keepdims=True)
        out_ref[r:r + 1, :] = m
        work = jnp.where(work == m, -jnp.inf, work)


def _peer_select_kernel(q_ref, keys_ref, s1_ref, s2_ref, e1_ref, e2_ref, tau_ref, a_ref, b_ref, sel_ref):
    k = PEER_TOPK
    sub = 8
    row_sub = lax.broadcasted_iota(jnp.int32, (sub, q_ref.shape[0]), 0)
    for h in range(PEER_HEADS):
        s = [_nt_dot(keys_ref[2 * h + c], q_ref[:, (2 * h + c) * LANES:(2 * h + c + 1) * LANES])
             for c in range(2)]
        _top_values(s[0], a_ref, k)
        _top_values(s[1], b_ref, k)
        a = a_ref[...]
        b = b_ref[...]
        groups = [a[0:1] + b]
        for r1 in range(1, sub):
            g = a[r1:r1 + 1] + b[0:sub]
            groups.append(jnp.where(row_sub < k // (r1 + 1), g, -jnp.inf))
        groups.append(a[sub:k] + b[0:1])
        _top_values(jnp.concatenate(groups, axis=0), sel_ref, k)
        sel = sel_ref[...]
        z = jnp.sum(jnp.exp(sel - sel[0:1]), axis=0, keepdims=True)
        s1_ref[h] = s[0]
        s2_ref[h] = s[1]
        e1_ref[h] = jnp.exp(s[0] - a[0:1]) * (0.5 / z)
        e2_ref[h] = jnp.exp(s[1] - b[0:1])
        tau_ref[h:h + 1, :] = sel[k - 1:k]


def _peer_select(q, keys):
    n = q.shape[0]
    t = PEER_SEL_TOKENS
    hk = jax.ShapeDtypeStruct((PEER_HEADS, PEER_NKEYS, n), F32)
    hk_spec = pl.BlockSpec((PEER_HEADS, PEER_NKEYS, t), lambda i: (0, 0, i))
    return pl.pallas_call(
        _peer_select_kernel,
        out_shape=(hk, hk, hk, hk, jax.ShapeDtypeStruct((PEER_HEADS, n), F32)),
        grid=(n // t,),
        in_specs=[pl.BlockSpec((t, q.shape[1]), lambda i: (i, 0)),
                  pl.BlockSpec(keys.shape, lambda i: (0, 0, 0))],
        out_specs=(hk_spec, hk_spec, hk_spec, hk_spec, pl.BlockSpec((PEER_HEADS, t), lambda i: (0, i))),
        scratch_shapes=[pltpu.VMEM((PEER_TOPK, t), F32)] * 3,
        compiler_params=_params("parallel"), name="peer_select",
    )(q, keys)


def _gelu_twice(a):
    return a * (1.0 + lax.erf(a * (1.0 / math.sqrt(2.0))))


def _peer_ffn_kernel(h_ref, u_ref, vt_ref, s1_ref, e1_ref, s2_ref, e2_ref, tau_ref, g_ref, b_ref,
                     o_ref, acc_ref, act_ref, w_ref, *, alpha):
    ec = pl.program_id(1)

    @pl.when(ec == 0)
    def _():
        acc_ref[...] = jnp.zeros(acc_ref.shape, F32)

    x = h_ref[...].astype(BF16)
    act_ref[...] = _nt_dot(u_ref[...], x)

    def rows(ii, carry):
        s1_rows = [s1_ref[h, pl.ds(ii, 1), :] for h in range(PEER_HEADS)]
        e1_rows = [e1_ref[h, pl.ds(ii, 1), :] for h in range(PEER_HEADS)]
        for pt in range(PEER_NKEYS // PEER_GATE_SLAB):
            js = slice(pt * PEER_GATE_SLAB, (pt + 1) * PEER_GATE_SLAB)
            gate = None
            for h in range(PEER_HEADS):
                pair_sum = s2_ref[h, js, :] + s1_rows[h]
                pair_gate = e2_ref[h, js, :] * e1_rows[h]
                term = jnp.where(pair_sum >= tau_ref[h:h + 1, :], pair_gate, 0.0)
                gate = term if gate is None else gate + term
            rs = pl.ds(pl.multiple_of(ii * PEER_NKEYS + pt * PEER_GATE_SLAB, PEER_GATE_SLAB), PEER_GATE_SLAB)
            w_ref[rs, :] = (gate * _gelu_twice(act_ref[rs, :])).astype(BF16)
        return carry

    lax.fori_loop(0, PEER_ROWS_PER_CHUNK, rows, 0)
    acc_ref[...] += _dot(vt_ref[...], w_ref[...])

    @pl.when(ec == pl.num_programs(1) - 1)
    def _():
        ffn = acc_ref[...].T
        o_ref[...] = _layer_norm(alpha * h_ref[...] + ffn, g_ref[...], b_ref[...])


def _peer_ffn(h, u, v_t, s1, s2, e1, e2, tau, g, b, alpha):
    n, d = h.shape
    t = PEER_TOKENS
    rows = PEER_ROWS_PER_CHUNK
    ec_size = rows * PEER_NKEYS
    n_chunks = u.shape[0] // ec_size
    tok = pl.BlockSpec((t, d), lambda i, e: (i, 0))
    chunk_rows = pl.BlockSpec((PEER_HEADS, rows, t), lambda i, e: (0, e, i))
    all_rows = pl.BlockSpec((PEER_HEADS, PEER_NKEYS, t), lambda i, e: (0, 0, i))
    vec = pl.BlockSpec((1, d), lambda i, e: (0, 0))
    return pl.pallas_call(
        functools.partial(_peer_ffn_kernel, alpha=alpha),
        out_shape=jax.ShapeDtypeStruct((n, d), F32),
        grid=(n // t, n_chunks),
        in_specs=[tok,
                  pl.BlockSpec((ec_size, d), lambda i, e: (e, 0)),
                  pl.BlockSpec((d, ec_size), lambda i, e: (0, e)),
                  chunk_rows, chunk_rows, all_rows, all_rows,
                  pl.BlockSpec((PEER_HEADS, t), lambda i, e: (0, i)),
                  vec, vec],
        out_specs=tok,
        scratch_shapes=[pltpu.VMEM((d, t), F32), pltpu.VMEM((ec_size, t), F32), pltpu.VMEM((ec_size, t), BF16)],
        compiler_params=_params("parallel", "arbitrary"), name="peer_ffn",
    )(h, u, v_t, s1, e1, s2, e2, tau, g.reshape(1, d), b.reshape(1, d))


def _cast_kernel(x_ref, o_ref, *, transpose):
    x = x_ref[...]
    o_ref[...] = (x.T if transpose else x).astype(o_ref.dtype)


def _cast_bf16(x, transpose=False):
    rows, cols = x.shape
    t = ROW_TILE
    out_shape, out_spec = ((cols, rows), pl.BlockSpec((cols, t), lambda i: (0, i))) if transpose else (
        (rows, cols), pl.BlockSpec((t, cols), lambda i: (i, 0)))
    return pl.pallas_call(
        functools.partial(_cast_kernel, transpose=transpose),
        out_shape=jax.ShapeDtypeStruct(out_shape, BF16),
        grid=(rows // t,),
        in_specs=[pl.BlockSpec((t, cols), lambda i: (i, 0))],
        out_specs=out_spec,
        compiler_params=_params("parallel"), name="cast_bf16")(x)


def _peer(h, w_q, sub_keys, u, v, g, b, alpha):
    q = _project(h, w_q.astype(BF16))
    keys = sub_keys.reshape(PEER_HEADS * 2, PEER_NKEYS, -1).astype(BF16)
    s1, s2, e1, e2, tau = _peer_select(q, keys)
    return _peer_ffn(h, _cast_bf16(u), _cast_bf16(v, transpose=True), s1, s2, e1, e2, tau, g, b, alpha)


def _t5_bucket(dist):
    max_exact = T5_BUCKETS // 2
    n = jnp.maximum(dist, 0)
    nf = jnp.maximum(n, 1).astype(F32)
    large = max_exact + (jnp.log(nf / max_exact) / math.log(T5_MAX_DIST / max_exact)
                         * (T5_BUCKETS - max_exact)).astype(jnp.int32)
    large = jnp.minimum(large, T5_BUCKETS - 1)
    return jnp.where(n < max_exact, n, large)


def _t5_near_tiles(t5_bias):
    key = jnp.arange(TK)[:, None]
    qry = jnp.arange(TQ)[None, :]
    tiles = []
    for off in range(2):
        dist = off * TK + qry - key
        bias = t5_bias[_t5_bucket(dist)] - t5_bias[T5_BUCKETS - 1]
        bias = jnp.where((dist >= 0)[..., None], bias, NEG)
        tiles.append(jnp.moveaxis(bias, -1, 0))
    return jnp.stack(tiles, axis=1).astype(F32)


def _dilated_log_multiplicity():
    key = np.arange(TK)[:, None]
    qry = np.arange(TQ)[None, :]
    tiles = []
    for off in range(N_DIL_OFFSETS):
        d = off * TK + qry - key
        count = np.zeros(d.shape, np.float64)
        for window, dil in DILATED_PATTERNS:
            count += (d >= 0) & (d <= window) & (d % dil == 0)
        tiles.append(np.where(count > 0, np.log(np.maximum(count, 1.0)), NEG))
    return np.stack(tiles).astype(np.float32)


def _dilated_tiles(t5_near):
    log_mult = jnp.asarray(_dilated_log_multiplicity())
    near = t5_near + log_mult[None, :2]
    far = jnp.broadcast_to(log_mult[None, 2:], (t5_near.shape[0],) + log_mult[2:].shape)
    return jnp.concatenate([near, far], axis=1)


def _attn_weights(w_in, gw, q_scales):
    grp = lambda g: w_in[:, g * gw:(g + 1) * gw]
    w_tok = jnp.concatenate([grp(1), grp(4)], axis=1).astype(BF16)
    w_feat_t = jnp.concatenate([grp(0) * q_scales[0], grp(2), grp(3) * q_scales[1], grp(5)], axis=1).T.astype(BF16)
    return w_tok, w_feat_t


def kernel(x, t5_bias, ab_w_in, ab_lambda, ab_subln_g, ab_w_out, cd_w_in, cd_b_f, cd_w_out,
           peer_w_q, peer_sub_keys, peer_u, peer_v, ln_g, ln_b):
    bsz, seq, d = x.shape
    depth = ln_g.shape[0]
    alpha = (2 * depth) ** 0.25
    n = bsz * seq
    gw = N_HEADS * HEAD_DIM
    blocks = gw // LANES
    h = x.reshape(n, d)
    t5_tiles = _t5_near_tiles(t5_bias)
    dil_tiles = _dilated_tiles(t5_tiles)
    for layer in range(depth):
        j = layer // 2
        if layer % 2 == 0:
            lambda_init = 0.8 - 0.6 * math.exp(-0.3 * layer)
            w_tok, w_feat_t = _attn_weights(ab_w_in[j], gw, (A_QK_DIM ** -0.5, HEAD_DIM ** -0.5))
            tok, feat = _project_attn(h, w_tok, w_feat_t)
            tok = tok.reshape(bsz, seq, -1)
            lv = ab_lambda[j].astype(F32)
            lam = jnp.exp(jnp.sum(lv[0] * lv[1])) - jnp.exp(jnp.sum(lv[2] * lv[3])) + lambda_init
            lam_row = jnp.full((1, TQ), lam, F32)
            g_tile = jnp.broadcast_to((ab_subln_g[j].astype(F32) * (1.0 - lambda_init))[:, None], (HEAD_DIM, TQ))
            o1 = _diff_attn(tok, feat, t5_tiles, lam_row, g_tile, 0, 0, blocks)
            o2 = _sb_attn(tok, feat, 2 * blocks, blocks, 3 * blocks)
            w_out = ab_w_out[j]
        else:
            w_tok, w_feat_t = _attn_weights(cd_w_in[j], gw, (HEAD_DIM ** -0.5, HEAD_DIM ** -0.5))
            n_gates = cd_w_in.shape[-1] - 6 * gw
            w_gate = jnp.zeros((d, LANES), F32).at[:, :n_gates].set(cd_w_in[j][:, 6 * gw:]).astype(BF16)
            tok, feat, gate = _project_attn(h, w_tok, w_feat_t, w_gate)
            tok = tok.reshape(bsz, seq, -1)
            f_rep = _fox_gates(gate, cd_b_f[j].astype(F32), bsz, seq)
            o1 = _dil_attn(tok, feat, dil_tiles, 0, 0, blocks)
            o2 = _fox_attn(tok, feat, f_rep, 2 * blocks, blocks, 3 * blocks)
            w_out = cd_w_out[j]
        h = _mix_ln(o1.reshape(n, gw), o2.reshape(n, gw), w_out.astype(BF16), h,
                    ln_g[layer, 0], ln_b[layer, 0], alpha)
        h = _peer(h, peer_w_q[layer], peer_sub_keys[layer], peer_u[layer], peer_v[layer],
                  ln_g[layer, 1], ln_b[layer, 1], alpha)
    return h.reshape(bsz, seq, d)
```

```python
import functools
import math

import numpy as np
import jax
import jax.numpy as jnp
from jax import lax
from jax.experimental import pallas as pl
from jax.experimental.pallas import tpu as pltpu

F32 = jnp.float32
BF16 = jnp.bfloat16

HEAD_DIM = 64
N_HEADS = 8
A_QK_DIM = 32
T5_BUCKETS = 32
T5_MAX_DIST = 128
DILATED_PATTERNS = ((128, 1), (512, 4), (2048, 16))
PEER_HEADS = 8
PEER_NKEYS = 128
PEER_TOPK = 16
LN_EPS = 1e-5

LANES = 128
HEADS_PER_TILE = LANES // HEAD_DIM
N_PAIRS = N_HEADS // HEADS_PER_TILE
TQ = 256
TK = 256
ROW_TILE = 512
GATE_TOKENS = 512
PEER_SEL_TOKENS = 256
PEER_TOKENS = 512
PEER_ROWS_PER_CHUNK = 8
VMEM_LIMIT_BYTES = 56 * 1024 * 1024
NEG = -1e30
SB_CUTOFF = -110.0


def _nt_dot(a, b):
    return lax.dot_general(a, b, (((1,), (1,)), ((), ())), preferred_element_type=F32)


def _dot(a, b):
    return jnp.dot(a, b, preferred_element_type=F32)


def _params(*sem):
    return pltpu.CompilerParams(dimension_semantics=sem, vmem_limit_bytes=VMEM_LIMIT_BYTES)


def _layer_norm(y, g, b):
    mu = jnp.mean(y, axis=-1, keepdims=True)
    yc = y - mu
    var = jnp.mean(yc * yc, axis=-1, keepdims=True)
    return yc * lax.rsqrt(var + LN_EPS) * g + b


def _log_sigmoid_pair(z):
    sp = jnp.log(1.0 + jnp.exp(-jnp.abs(z)))
    return jnp.minimum(z, 0.0) - sp, -jnp.maximum(z, 0.0) - sp


def _proj_kernel(x_ref, w_ref, o_ref, *, tn):
    x = x_ref[...].astype(BF16)
    for c in range(w_ref.shape[1] // tn):
        o_ref[:, c * tn:(c + 1) * tn] = _dot(x, w_ref[:, c * tn:(c + 1) * tn]).astype(o_ref.dtype)


def _project(x, w, tn=512):
    n, k = x.shape
    m = w.shape[1]
    tm = ROW_TILE
    return pl.pallas_call(
        functools.partial(_proj_kernel, tn=tn),
        out_shape=jax.ShapeDtypeStruct((n, m), BF16),
        grid=(n // tm,),
        in_specs=[pl.BlockSpec((tm, k), lambda i: (i, 0)), pl.BlockSpec((k, m), lambda i: (0, 0))],
        out_specs=pl.BlockSpec((tm, m), lambda i: (i, 0)),
        compiler_params=_params("parallel"), name="proj")(x, w)


def _proj_attn_kernel(x_ref, wt_ref, wf_ref, *rest, tn, has_gate):
    if has_gate:
        wg_ref, tok_ref, feat_ref, gate_ref = rest
    else:
        tok_ref, feat_ref = rest
    x = x_ref[...].astype(BF16)
    for c in range(wt_ref.shape[1] // tn):
        sl = slice(c * tn, (c + 1) * tn)
        tok_ref[:, sl] = _dot(x, wt_ref[:, sl]).astype(BF16)
    for c in range(wf_ref.shape[0] // tn):
        sl = slice(c * tn, (c + 1) * tn)
        feat_ref[sl, :] = _nt_dot(wf_ref[sl, :], x).astype(BF16)
    if has_gate:
        gate_ref[...] = _dot(x, wg_ref[...])


def _project_attn(x, w_tok, w_feat_t, w_gate=None, tn=512):
    n, k = x.shape
    mt, mf = w_tok.shape[1], w_feat_t.shape[0]
    tm = ROW_TILE
    has_gate = w_gate is not None
    in_specs = [pl.BlockSpec((tm, k), lambda i: (i, 0)),
                pl.BlockSpec((k, mt), lambda i: (0, 0)),
                pl.BlockSpec((mf, k), lambda i: (0, 0))]
    out_shape = [jax.ShapeDtypeStruct((n, mt), BF16), jax.ShapeDtypeStruct((mf, n), BF16)]
    out_specs = [pl.BlockSpec((tm, mt), lambda i: (i, 0)), pl.BlockSpec((mf, tm), lambda i: (0, i))]
    args = [x, w_tok, w_feat_t]
    if has_gate:
        in_specs.append(pl.BlockSpec((k, LANES), lambda i: (0, 0)))
        out_shape.append(jax.ShapeDtypeStruct((n, LANES), F32))
        out_specs.append(pl.BlockSpec((tm, LANES), lambda i: (i, 0)))
        args.append(w_gate)
    return pl.pallas_call(
        functools.partial(_proj_attn_kernel, tn=tn, has_gate=has_gate),
        out_shape=tuple(out_shape), grid=(n // tm,), in_specs=in_specs, out_specs=tuple(out_specs),
        compiler_params=_params("parallel"), name="proj_attn")(*args)


def _mix_ln_kernel(oa_ref, ob_ref, wa_ref, wb_ref, h_ref, g_ref, b_ref, o_ref, *, alpha):
    mix = _dot(oa_ref[...], wa_ref[...]) + _dot(ob_ref[...], wb_ref[...])
    o_ref[...] = _layer_norm(alpha * h_ref[...] + mix, g_ref[...], b_ref[...])


def _mix_ln(oa, ob, w_out, h, g, b, alpha):
    n, d = h.shape
    wa_rows = oa.shape[1]
    tm = ROW_TILE
    row = lambda width: pl.BlockSpec((tm, width), lambda i: (i, 0))
    full = lambda r, c: pl.BlockSpec((r, c), lambda i: (0, 0))
    return pl.pallas_call(
        functools.partial(_mix_ln_kernel, alpha=alpha),
        out_shape=jax.ShapeDtypeStruct((n, d), F32),
        grid=(n // tm,),
        in_specs=[row(wa_rows), row(ob.shape[1]), full(wa_rows, d), full(ob.shape[1], d),
                  row(d), full(1, d), full(1, d)],
        out_specs=row(d),
        compiler_params=_params("parallel"), name="mix_ln",
    )(oa, ob, w_out[:wa_rows], w_out[wa_rows:], h, g.reshape(1, d), b.reshape(1, d))


def _row_group_mask(shape, width, index):
    row = lax.broadcasted_iota(jnp.int32, shape, 0)
    return (row // width) == index


def _masked_queries(qt_ref, width):
    qt = qt_ref[...].astype(F32)
    return [jnp.where(_row_group_mask(qt.shape, width, j), qt, 0.0).astype(BF16)
            for j in range(LANES // width)]


def _score_fn(k_ref, qm):
    def qk(j, s_ref):
        k = k_ref[pl.ds(pl.multiple_of(j * TK, TK), TK), :]
        for mp, q in enumerate(qm):
            s_ref[mp] = _dot(k, q)
    return qk


def _alternate(lo, hi, bufs, step):
    def pair(t, carry):
        i = lo + 2 * t
        step(i, bufs[0], bufs[1])

        @pl.when(i + 1 < hi)
        def _():
            step(i + 1, bufs[1], bufs[0])
        return carry

    lax.fori_loop(0, (hi - lo + 1) // 2, pair, 0)


def _flash_block(i, first_block, n_blocks, qk, s_read, s_write, vt_ref, heads_of_map, adjust,
                 m_ref, l_ref, acc_ref):
    qk(first_block - jnp.minimum(i + 1, n_blocks - 1), s_write)
    start = pl.multiple_of((first_block - i) * TK, TK)
    vt = vt_ref[:, pl.ds(start, TK)]
    for mp, h in enumerate(heads_of_map):
        st = adjust(mp, s_read[mp], start)
        m_old = m_ref[mp]
        m_new = jnp.maximum(m_old, jnp.max(st, axis=0, keepdims=True))
        alpha = jnp.exp(m_old - m_new)
        p = jnp.exp(st - m_new)
        l_ref[mp] = alpha * l_ref[mp] + jnp.sum(p, axis=0, keepdims=True)
        pv = _dot(vt[h * HEAD_DIM:(h + 1) * HEAD_DIM, :], p.astype(BF16))
        acc_ref[mp] = alpha * acc_ref[mp] + pv
        m_ref[mp] = m_new


def _attn_specs(q_row, k_col, v_row, seq):
    nq = seq // TQ
    q_spec = pl.BlockSpec((LANES, TQ), lambda b, hp, qi: (q_row + hp, b * nq + qi))
    k_spec = pl.BlockSpec((None, seq, LANES), lambda b, hp, qi: (b, 0, k_col + hp))
    v_spec = pl.BlockSpec((LANES, seq), lambda b, hp, qi: (v_row + hp, b))
    o_spec = pl.BlockSpec((None, TQ, LANES), lambda b, hp, qi: (b, qi, hp))
    return q_spec, k_spec, v_spec, o_spec


def _attn_out_shape(tok):
    return jax.ShapeDtypeStruct((tok.shape[0], tok.shape[1], N_HEADS * HEAD_DIM), BF16)


def _softmax_scratch(n):
    return [pltpu.VMEM((n, TK, TQ), F32), pltpu.VMEM((n, TK, TQ), F32), pltpu.VMEM((n, 1, TQ), F32),
            pltpu.VMEM((n, 1, TQ), F32), pltpu.VMEM((n, HEAD_DIM, TQ), F32)]


def _init_softmax(m_ref, l_ref, acc_ref):
    m_ref[...] = jnp.full(m_ref.shape, NEG, F32)
    l_ref[...] = jnp.zeros(l_ref.shape, F32)
    acc_ref[...] = jnp.zeros(acc_ref.shape, F32)


def _store_heads(o_ref, head_rows):
    o_ref[...] = jnp.concatenate(head_rows, axis=0).T.astype(o_ref.dtype)


_ATTN_SEM = ("parallel", "parallel", "arbitrary")


def _diff_attn_kernel(qt_ref, k_ref, vt_ref, bias_ref, lam_ref, g_ref, o_ref, s_a, s_b, m_ref, l_ref, acc_ref):
    qi = pl.program_id(2)
    qk = _score_fn(k_ref, _masked_queries(qt_ref, A_QK_DIM))
    heads_of_map = (0, 0, 1, 1)
    _init_softmax(m_ref, l_ref, acc_ref)
    n_blocks = qi + 1
    qk(qi, s_a)

    def block(i, s_read, s_write, adjust):
        _flash_block(i, qi, n_blocks, qk, s_read, s_write, vt_ref, heads_of_map, adjust, m_ref, l_ref, acc_ref)

    block(0, s_a, s_b, lambda mp, st, start: st + bias_ref[mp // 2, 0])

    @pl.when(qi >= 1)
    def _():
        block(1, s_b, s_a, lambda mp, st, start: st + bias_ref[mp // 2, 1])

    _alternate(2, n_blocks, (s_a, s_b), lambda i, r, w: block(i, r, w, lambda mp, st, start: st))

    lam = lam_ref[...]
    heads = []
    for h in range(HEADS_PER_TILE):
        o = acc_ref[2 * h] / l_ref[2 * h] - lam * (acc_ref[2 * h + 1] / l_ref[2 * h + 1])
        ms = jnp.mean(o * o, axis=0, keepdims=True)
        heads.append(o * lax.rsqrt(ms + LN_EPS) * g_ref[...])
    _store_heads(o_ref, heads)


def _diff_attn(tok, feat, bias_tiles, lam_row, g_tile, q_row, k_col, v_row):
    b, s, _ = tok.shape
    q_spec, k_spec, v_spec, o_spec = _attn_specs(q_row, k_col, v_row, s)
    bias_spec = pl.BlockSpec((HEADS_PER_TILE, 2, TK, TQ), lambda b, hp, qi: (hp, 0, 0, 0))
    return pl.pallas_call(
        _diff_attn_kernel, out_shape=_attn_out_shape(tok),
        grid=(b, N_PAIRS, s // TQ),
        in_specs=[q_spec, k_spec, v_spec, bias_spec,
                  pl.BlockSpec((1, TQ), lambda b, hp, qi: (0, 0)),
                  pl.BlockSpec((HEAD_DIM, TQ), lambda b, hp, qi: (0, 0))],
        out_specs=o_spec, scratch_shapes=_softmax_scratch(LANES // A_QK_DIM),
        compiler_params=_params(*_ATTN_SEM), name="diff_attn",
    )(feat, tok, feat, bias_tiles, lam_row, g_tile)


def _sb_attn_kernel(qt_ref, k_ref, vt_ref, o_ref, s_a, s_b, c_ref, acc_ref):
    qi = pl.program_id(2)
    qk = _score_fn(k_ref, _masked_queries(qt_ref, HEAD_DIM))
    key = lax.broadcasted_iota(jnp.int32, (TK, TQ), 0)
    qry = lax.broadcasted_iota(jnp.int32, (TK, TQ), 1)
    past = key < qry
    later = (qry > key).astype(BF16)
    c_ref[...] = jnp.zeros(c_ref.shape, F32)
    acc_ref[...] = jnp.zeros(acc_ref.shape, F32)
    qk(qi, s_a)

    def block(i, s_read, s_write, diag):
        j = qi - i
        qk(jnp.maximum(j - 1, 0), s_write)
        vt = vt_ref[:, pl.ds(pl.multiple_of(j * TK, TK), TK)]
        c_max = []
        for h in range(HEADS_PER_TILE):
            log_beta, log_rem = _log_sigmoid_pair(s_read[h])
            if diag:
                log_rem = jnp.where(past, log_rem, 0.0)
            hi = log_rem.astype(BF16)
            lo = (log_rem - hi.astype(F32)).astype(BF16)
            after = _dot(later, hi) + _dot(later, lo) + c_ref[h]
            w = jnp.exp(log_beta + after)
            if diag:
                w = jnp.where(past, w, 0.0)
            acc_ref[h] += _dot(vt[h * HEAD_DIM:(h + 1) * HEAD_DIM, :], w.astype(BF16))
            c_new = c_ref[h] + jnp.sum(log_rem, axis=0, keepdims=True)
            c_ref[h] = c_new
            c_max.append(jnp.max(c_new))
        return jnp.maximum(c_max[0], c_max[1])

    first = block(0, s_a, s_b, True)

    def cond(state):
        return jnp.logical_and(state[0] <= qi, state[1] > SB_CUTOFF)

    def body(state):
        i = state[0]
        c_first = block(i, s_b, s_a, False)
        more = jnp.logical_and(i + 1 <= qi, c_first > SB_CUTOFF)
        c_second = lax.cond(more, lambda: block(i + 1, s_a, s_b, False), lambda: c_first)
        return i + 2, c_second

    lax.while_loop(cond, body, (jnp.int32(1), first))
    _store_heads(o_ref, [acc_ref[h] for h in range(HEADS_PER_TILE)])


def _sb_attn(tok, feat, q_row, k_col, v_row):
    b, s, _ = tok.shape
    q_spec, k_spec, v_spec, o_spec = _attn_specs(q_row, k_col, v_row, s)
    return pl.pallas_call(
        _sb_attn_kernel, out_shape=_attn_out_shape(tok),
        grid=(b, N_PAIRS, s // TQ),
        in_specs=[q_spec, k_spec, v_spec], out_specs=o_spec,
        scratch_shapes=[pltpu.VMEM((HEADS_PER_TILE, TK, TQ), F32), pltpu.VMEM((HEADS_PER_TILE, TK, TQ), F32),
                        pltpu.VMEM((HEADS_PER_TILE, 1, TQ), F32), pltpu.VMEM((HEADS_PER_TILE, HEAD_DIM, TQ), F32)],
        compiler_params=_params(*_ATTN_SEM), name="sb_attn",
    )(feat, tok, feat)


N_DIL_OFFSETS = max(w for w, _ in DILATED_PATTERNS) // TK + 1


def _dil_attn_kernel(qt_ref, k_ref, vt_ref, tile_ref, o_ref, s_a, s_b, m_ref, l_ref, acc_ref):
    qi = pl.program_id(2)
    qk = _score_fn(k_ref, _masked_queries(qt_ref, HEAD_DIM))
    _init_softmax(m_ref, l_ref, acc_ref)
    n_blocks = jnp.minimum(qi, N_DIL_OFFSETS - 1) + 1
    qk(qi, s_a)

    def block(i, s_read, s_write):
        _flash_block(i, qi, n_blocks, qk, s_read, s_write, vt_ref, (0, 1),
                     lambda mp, st, start: st + tile_ref[mp, i], m_ref, l_ref, acc_ref)

    _alternate(0, n_blocks, (s_a, s_b), block)
    _store_heads(o_ref, [acc_ref[h] / l_ref[h] for h in range(HEADS_PER_TILE)])


def _dil_attn(tok, feat, tiles, q_row, k_col, v_row):
    b, s, _ = tok.shape
    q_spec, k_spec, v_spec, o_spec = _attn_specs(q_row, k_col, v_row, s)
    tile_spec = pl.BlockSpec((HEADS_PER_TILE, N_DIL_OFFSETS, TK, TQ), lambda b, hp, qi: (hp, 0, 0, 0))
    return pl.pallas_call(
        _dil_attn_kernel, out_shape=_attn_out_shape(tok),
        grid=(b, N_PAIRS, s // TQ),
        in_specs=[q_spec, k_spec, v_spec, tile_spec],
        out_specs=o_spec, scratch_shapes=_softmax_scratch(HEADS_PER_TILE),
        compiler_params=_params(*_ATTN_SEM), name="dil_attn",
    )(feat, tok, feat, tiles)


def _fox_gate_kernel(g_ref, bf_ref, f_ref, carry_ref):
    @pl.when(pl.program_id(1) == 0)
    def _():
        carry_ref[...] = jnp.zeros(carry_ref.shape, F32)

    row = lax.broadcasted_iota(jnp.int32, (LANES, LANES), 0)
    col = lax.broadcasted_iota(jnp.int32, (LANES, LANES), 1)
    upto = (col <= row).astype(F32)
    exact = functools.partial(jnp.dot, preferred_element_type=F32, precision=lax.Precision.HIGHEST)
    for blk in range(g_ref.shape[0] // LANES):
        sl = slice(blk * LANES, (blk + 1) * LANES)
        log_f, _ = _log_sigmoid_pair(g_ref[sl, :] + bf_ref[...])
        cs = exact(upto, log_f) + carry_ref[...]
        carry_ref[...] = cs[LANES - 1:LANES, :]
        for h in range(f_ref.shape[0]):
            f_ref[h, sl, :] = exact(cs, (row == h).astype(F32))


def _fox_gates(gate, b_f, bsz, seq):
    n = gate.shape[0]
    heads = b_f.shape[0]
    t = GATE_TOKENS
    nt = seq // t
    bf_row = jnp.zeros((1, LANES), F32).at[0, :heads].set(b_f)
    return pl.pallas_call(
        _fox_gate_kernel, out_shape=jax.ShapeDtypeStruct((heads, n, LANES), F32),
        grid=(bsz, nt),
        in_specs=[pl.BlockSpec((t, LANES), lambda b, i: (b * nt + i, 0)),
                  pl.BlockSpec((1, LANES), lambda b, i: (0, 0))],
        out_specs=pl.BlockSpec((heads, t, LANES), lambda b, i: (0, b * nt + i, 0)),
        scratch_shapes=[pltpu.VMEM((1, LANES), F32)],
        compiler_params=_params("parallel", "arbitrary"), name="fox_gates",
    )(gate, bf_row)


def _fox_attn_kernel(qt_ref, k_ref, vt_ref, f_ref, o_ref, s_a, s_b, m_ref, l_ref, acc_ref):
    qi = pl.program_id(2)
    qk = _score_fn(k_ref, _masked_queries(qt_ref, HEAD_DIM))
    _init_softmax(m_ref, l_ref, acc_ref)
    n_blocks = qi + 1
    key = lax.broadcasted_iota(jnp.int32, (TK, TQ), 0)
    qry = lax.broadcasted_iota(jnp.int32, (TK, TQ), 1)
    causal = key <= qry
    qk(qi, s_a)

    def forget(mp, st, start):
        f = f_ref[mp, pl.ds(start, TK), :]
        return st - jnp.concatenate([f] * (TQ // LANES), axis=1)

    def block(i, s_read, s_write, adjust):
        _flash_block(i, qi, n_blocks, qk, s_read, s_write, vt_ref, (0, 1), adjust, m_ref, l_ref, acc_ref)

    block(0, s_a, s_b, lambda mp, st, start: jnp.where(causal, forget(mp, st, start), NEG))
    _alternate(1, n_blocks, (s_b, s_a), lambda i, r, w: block(i, r, w, forget))
    _store_heads(o_ref, [acc_ref[h] / l_ref[h] for h in range(HEADS_PER_TILE)])


def _fox_attn(tok, feat, f_rep, q_row, k_col, v_row):
    b, s, _ = tok.shape
    q_spec, k_spec, v_spec, o_spec = _attn_specs(q_row, k_col, v_row, s)
    f_spec = pl.BlockSpec((HEADS_PER_TILE, s, LANES), lambda b, hp, qi: (hp, b, 0))
    return pl.pallas_call(
        _fox_attn_kernel, out_shape=_attn_out_shape(tok),
        grid=(b, N_PAIRS, s // TQ),
        in_specs=[q_spec, k_spec, v_spec, f_spec],
        out_specs=o_spec, scratch_shapes=_softmax_scratch(HEADS_PER_TILE),
        compiler_params=_params(*_ATTN_SEM), name="fox_attn",
    )(feat, tok, feat, f_rep)


def _top_values(work, out_ref, count):
    for r in range(count):
        m = jnp.max(work, axis=0, keepdims=True)
        out_ref[r:r + 1, :] = m
        work = jnp.where(work == m, -jnp.inf, work)


def _peer_select_kernel(q_ref, keys_ref, s1_ref, s2_ref, e1_ref, e2_ref, tau_ref, a_ref, b_ref, sel_ref):
    k = PEER_TOPK
    sub = 8
    row_sub = lax.broadcasted_iota(jnp.int32, (sub, q_ref.shape[0]), 0)
    for h in range(PEER_HEADS):
        s = [_nt_dot(keys_ref[2 * h + c], q_ref[:, (2 * h + c) * LANES:(2 * h + c + 1) * LANES])
             for c in range(2)]
        _top_values(s[0], a_ref, k)
        _top_values(s[1], b_ref, k)
        a = a_ref[...]
        b = b_ref[...]
        groups = [a[0:1] + b]
        for r1 in range(1, sub):
            g = a[r1:r1 + 1] + b[0:sub]
            groups.append(jnp.where(row_sub < k // (r1 + 1), g, -jnp.inf))
        groups.append(a[sub:k] + b[0:1])
        _top_values(jnp.concatenate(groups, axis=0), sel_ref, k)
        sel = sel_ref[...]
        z = jnp.sum(jnp.exp(sel - sel[0:1]), axis=0, keepdims=True)
        s1_ref[h] = s[0]
        s2_ref[h] = s[1]
        e1_ref[h] = jnp.exp(s[0] - a[0:1]) / z
        e2_ref[h] = jnp.exp(s[1] - b[0:1])
        tau_ref[h:h + 1, :] = sel[k - 1:k]


def _peer_select(q, keys):
    n = q.shape[0]
    t = PEER_SEL_TOKENS
    hk = jax.ShapeDtypeStruct((PEER_HEADS, PEER_NKEYS, n), F32)
    hk_spec = pl.BlockSpec((PEER_HEADS, PEER_NKEYS, t), lambda i: (0, 0, i))
    return pl.pallas_call(
        _peer_select_kernel,
        out_shape=(hk, hk, hk, hk, jax.ShapeDtypeStruct((PEER_HEADS, n), F32)),
        grid=(n // t,),
        in_specs=[pl.BlockSpec((t, q.shape[1]), lambda i: (i, 0)),
                  pl.BlockSpec(keys.shape, lambda i: (0, 0, 0))],
        out_specs=(hk_spec, hk_spec, hk_spec, hk_spec, pl.BlockSpec((PEER_HEADS, t), lambda i: (0, i))),
        scratch_shapes=[pltpu.VMEM((PEER_TOPK, t), F32)] * 3,
        compiler_params=_params("parallel"), name="peer_select",
    )(q, keys)


def _gelu(a):
    return 0.5 * a * (1.0 + lax.erf(a * (1.0 / math.sqrt(2.0))))


def _peer_ffn_kernel(h_ref, u_ref, vt_ref, s1_ref, e1_ref, s2_ref, e2_ref, tau_ref, g_ref, b_ref,
                     o_ref, acc_ref, act_ref, w_ref, *, alpha):
    ec = pl.program_id(1)

    @pl.when(ec == 0)
    def _():
        acc_ref[...] = jnp.zeros(acc_ref.shape, F32)

    x = h_ref[...].astype(BF16)
    act_ref[...] = _nt_dot(u_ref[...], x)

    def rows(ii, carry):
        sl = pl.ds(pl.multiple_of(ii * PEER_NKEYS, PEER_NKEYS), PEER_NKEYS)
        gate = jnp.zeros((PEER_NKEYS, x.shape[0]), F32)
        for h in range(PEER_HEADS):
            pair_sum = s2_ref[h] + s1_ref[h, pl.ds(ii, 1), :]
            pair_gate = e2_ref[h] * e1_ref[h, pl.ds(ii, 1), :]
            gate = gate + jnp.where(pair_sum >= tau_ref[h:h + 1, :], pair_gate, 0.0)
        w_ref[sl, :] = (gate * _gelu(act_ref[sl, :])).astype(BF16)
        return carry

    lax.fori_loop(0, PEER_ROWS_PER_CHUNK, rows, 0)
    acc_ref[...] += _dot(vt_ref[...], w_ref[...])

    @pl.when(ec == pl.num_programs(1) - 1)
    def _():
        ffn = acc_ref[...].T
        o_ref[...] = _layer_norm(alpha * h_ref[...] + ffn, g_ref[...], b_ref[...])


def _peer_ffn(h, u, v_t, s1, s2, e1, e2, tau, g, b, alpha):
    n, d = h.shape
    t = PEER_TOKENS
    rows = PEER_ROWS_PER_CHUNK
    ec_size = rows * PEER_NKEYS
    n_chunks = u.shape[0] // ec_size
    tok = pl.BlockSpec((t, d), lambda i, e: (i, 0))
    chunk_rows = pl.BlockSpec((PEER_HEADS, rows, t), lambda i, e: (0, e, i))
    all_rows = pl.BlockSpec((PEER_HEADS, PEER_NKEYS, t), lambda i, e: (0, 0, i))
    vec = pl.BlockSpec((1, d), lambda i, e: (0, 0))
    return pl.pallas_call(
        functools.partial(_peer_ffn_kernel, alpha=alpha),
        out_shape=jax.ShapeDtypeStruct((n, d), F32),
        grid=(n // t, n_chunks),
        in_specs=[tok,
                  pl.BlockSpec((ec_size, d), lambda i, e: (e, 0)),
                  pl.BlockSpec((d, ec_size), lambda i, e: (0, e)),
                  chunk_rows, chunk_rows, all_rows, all_rows,
                  pl.BlockSpec((PEER_HEADS, t), lambda i, e: (0, i)),
                  vec, vec],
        out_specs=tok,
        scratch_shapes=[pltpu.VMEM((d, t), F32), pltpu.VMEM((ec_size, t), F32), pltpu.VMEM((ec_size, t), BF16)],
        compiler_params=_params("parallel", "arbitrary"), name="peer_ffn",
    )(h, u, v_t, s1, e1, s2, e2, tau, g.reshape(1, d), b.reshape(1, d))


def _cast_kernel(x_ref, o_ref, *, transpose):
    x = x_ref[...]
    o_ref[...] = (x.T if transpose else x).astype(o_ref.dtype)


def _cast_bf16(x, layer, transpose=False):
    _, rows, cols = x.shape
    t = ROW_TILE
    out_shape, out_spec = ((cols, rows), pl.BlockSpec((cols, t), lambda i: (0, i))) if transpose else (
        (rows, cols), pl.BlockSpec((t, cols), lambda i: (i, 0)))
    return pl.pallas_call(
        functools.partial(_cast_kernel, transpose=transpose),
        out_shape=jax.ShapeDtypeStruct(out_shape, BF16),
        grid=(rows // t,),
        in_specs=[pl.BlockSpec((None, t, cols), lambda i: (layer, i, 0))],
        out_specs=out_spec,
        compiler_params=_params("parallel"), name="cast_bf16")(x)


def _peer(h, w_q, sub_keys, u_all, v_all, layer, g, b, alpha):
    q = _project(h, w_q.astype(BF16))
    keys = sub_keys.reshape(PEER_HEADS * 2, PEER_NKEYS, -1).astype(BF16)
    s1, s2, e1, e2, tau = _peer_select(q, keys)
    u = _cast_bf16(u_all, layer)
    v_t = _cast_bf16(v_all, layer, transpose=True)
    return _peer_ffn(h, u, v_t, s1, s2, e1, e2, tau, g, b, alpha)


def _t5_bucket(dist):
    max_exact = T5_BUCKETS // 2
    n = jnp.maximum(dist, 0)
    nf = jnp.maximum(n, 1).astype(F32)
    large = max_exact + (jnp.log(nf / max_exact) / math.log(T5_MAX_DIST / max_exact)
                         * (T5_BUCKETS - max_exact)).astype(jnp.int32)
    large = jnp.minimum(large, T5_BUCKETS - 1)
    return jnp.where(n < max_exact, n, large)


def _t5_near_tiles(t5_bias):
    key = jnp.arange(TK)[:, None]
    qry = jnp.arange(TQ)[None, :]
    tiles = []
    for off in range(2):
        dist = off * TK + qry - key
        bias = t5_bias[_t5_bucket(dist)] - t5_bias[T5_BUCKETS - 1]
        bias = jnp.where((dist >= 0)[..., None], bias, NEG)
        tiles.append(jnp.moveaxis(bias, -1, 0))
    return jnp.stack(tiles, axis=1).astype(F32)


def _dilated_log_multiplicity():
    key = np.arange(TK)[:, None]
    qry = np.arange(TQ)[None, :]
    tiles = []
    for off in range(N_DIL_OFFSETS):
        d = off * TK + qry - key
        count = np.zeros(d.shape, np.float64)
        for window, dil in DILATED_PATTERNS:
            count += (d >= 0) & (d <= window) & (d % dil == 0)
        tiles.append(np.where(count > 0, np.log(np.maximum(count, 1.0)), NEG))
    return np.stack(tiles).astype(np.float32)


def _dilated_tiles(t5_near):
    log_mult = jnp.asarray(_dilated_log_multiplicity())
    near = t5_near + log_mult[None, :2]
    far = jnp.broadcast_to(log_mult[None, 2:], (t5_near.shape[0],) + log_mult[2:].shape)
    return jnp.concatenate([near, far], axis=1)


def _attn_weights(w_in, gw, q_scales):
    grp = lambda g: w_in[:, g * gw:(g + 1) * gw]
    w_tok = jnp.concatenate([grp(1), grp(4)], axis=1).astype(BF16)
    w_feat_t = jnp.concatenate([grp(0) * q_scales[0], grp(2), grp(3) * q_scales[1], grp(5)], axis=1).T.astype(BF16)
    return w_tok, w_feat_t


def kernel(x, t5_bias, ab_w_in, ab_lambda, ab_subln_g, ab_w_out, cd_w_in, cd_b_f, cd_w_out,
           peer_w_q, peer_sub_keys, peer_u, peer_v, ln_g, ln_b):
    bsz, seq, d = x.shape
    depth = ln_g.shape[0]
    alpha = (2 * depth) ** 0.25
    n = bsz * seq
    gw = N_HEADS * HEAD_DIM
    blocks = gw // LANES
    h = x.reshape(n, d)
    t5_tiles = _t5_near_tiles(t5_bias)
    dil_tiles = _dilated_tiles(t5_tiles)
    for layer in range(depth):
        j = layer // 2
        if layer % 2 == 0:
            lambda_init = 0.8 - 0.6 * math.exp(-0.3 * layer)
            w_tok, w_feat_t = _attn_weights(ab_w_in[j], gw, (A_QK_DIM ** -0.5, HEAD_DIM ** -0.5))
            tok, feat = _project_attn(h, w_tok, w_feat_t)
            tok = tok.reshape(bsz, seq, -1)
            lv = ab_lambda[j].astype(F32)
            lam = jnp.exp(jnp.sum(lv[0] * lv[1])) - jnp.exp(jnp.sum(lv[2] * lv[3])) + lambda_init
            lam_row = jnp.full((1, TQ), lam, F32)
            g_tile = jnp.broadcast_to((ab_subln_g[j].astype(F32) * (1.0 - lambda_init))[:, None], (HEAD_DIM, TQ))
            o1 = _diff_attn(tok, feat, t5_tiles, lam_row, g_tile, 0, 0, blocks)
            o2 = _sb_attn(tok, feat, 2 * blocks, blocks, 3 * blocks)
            w_out = ab_w_out[j]
        else:
            w_tok, w_feat_t = _attn_weights(cd_w_in[j], gw, (HEAD_DIM ** -0.5, HEAD_DIM ** -0.5))
            n_gates = cd_w_in.shape[-1] - 6 * gw
            w_gate = jnp.zeros((d, LANES), F32).at[:, :n_gates].set(cd_w_in[j][:, 6 * gw:]).astype(BF16)
            tok, feat, gate = _project_attn(h, w_tok, w_feat_t, w_gate)
            tok = tok.reshape(bsz, seq, -1)
            f_rep = _fox_gates(gate, cd_b_f[j].astype(F32), bsz, seq)
            o1 = _dil_attn(tok, feat, dil_tiles, 0, 0, blocks)
            o2 = _fox_attn(tok, feat, f_rep, 2 * blocks, blocks, 3 * blocks)
            w_out = cd_w_out[j]
        h = _mix_ln(o1.reshape(n, gw), o2.reshape(n, gw), w_out.astype(BF16), h,
                    ln_g[layer, 0], ln_b[layer, 0], alpha)
        h = _peer(h, peer_w_q[layer], peer_sub_keys[layer], peer_u, peer_v, layer,
                  ln_g[layer, 1], ln_b[layer, 1], alpha)
    return h.reshape(bsz, seq, d)
```
